```python
import math
import jax
import jax.numpy as jnp
from jax import lax
import numpy as np

D_MODEL = 1024
BATCH = 8
SEQ = 2048
DEPTH = 2
DEC_BATCH = 128
DEC_SEQ = 4
PAST_LEN = 16384
PAGE_SIZE = 128

N_MIXERS = 2
EXPAND = 2
D_INNER = EXPAND * D_MODEL
SSM_GROUP = 16
SSM_GROUPS = D_INNER // SSM_GROUP
SSM_STATE = 64
CONV_WIDTH = 31
N_SSM_LAYERS = (DEPTH + 1) // 2
N_CONV_LAYERS = DEPTH // 2
EPS = 1e-6
DT_MIN = 1e-3
DT_MAX = 1e-1

kernel_name = 'hybrid_s5_conformer_conv_decode_step'


def rms_norm(x, g):
    xf = x.astype(jnp.float32)
    y = xf * lax.rsqrt(jnp.mean(xf * xf, axis=-1, keepdims=True) + EPS)
    return (y * g.astype(jnp.float32)).astype(x.dtype)


def layer_norm(x, g, b):
    xf = x.astype(jnp.float32)
    mu = jnp.mean(xf, axis=-1, keepdims=True)
    xc = xf - mu
    y = xc * lax.rsqrt(jnp.mean(xc * xc, axis=-1, keepdims=True) + EPS)
    return (y * g.astype(jnp.float32) + b.astype(jnp.float32)).astype(x.dtype)


def _ssm_combine(e1, e2):
    a1r, a1i, b1r, b1i = e1
    a2r, a2i, b2r, b2i = e2
    return (a2r * a1r - a2i * a1i,
            a2r * a1i + a2i * a1r,
            a2r * b1r - a2i * b1i + b2r,
            a2r * b1i + a2i * b1r + b2i)


def ssm_scan(u, h0_re, h0_im, a_re, a_im, log_dt, b_re, b_im, c_re, c_im, d_skip):
    f32 = jnp.float32
    n, t, _ = u.shape
    u = u.astype(f32)
    a_re = a_re.astype(f32)
    a_im = a_im.astype(f32)
    dt = jnp.exp(log_dt.astype(f32))[:, None]
    mag = jnp.exp(a_re * dt)
    lb_re = mag * jnp.cos(a_im * dt)
    lb_im = mag * jnp.sin(a_im * dt)
    den = a_re * a_re + a_im * a_im
    f_re = ((lb_re - 1.0) * a_re + lb_im * a_im) / den
    f_im = (lb_im * a_re - (lb_re - 1.0) * a_im) / den
    b_re = b_re.astype(f32)
    b_im = b_im.astype(f32)
    bb_re = f_re[..., None] * b_re - f_im[..., None] * b_im
    bb_im = f_re[..., None] * b_im + f_im[..., None] * b_re
    ug = u.reshape(n, t, SSM_GROUPS, SSM_GROUP)
    bu_re = jnp.einsum('ntgh,gph->tngp', ug, bb_re)
    bu_im = jnp.einsum('ntgh,gph->tngp', ug, bb_im)
    la_re = jnp.broadcast_to(lb_re, (t, 1) + lb_re.shape)
    la_im = jnp.broadcast_to(lb_im, (t, 1) + lb_im.shape)
    pa_re, pa_im, h_re, h_im = lax.associative_scan(
        _ssm_combine, (la_re, la_im, bu_re, bu_im), axis=0)
    if h0_re is not None:
        h0_re = h0_re.astype(f32)
        h0_im = h0_im.astype(f32)
        h_re, h_im = (h_re + pa_re * h0_re - pa_im * h0_im,
                      h_im + pa_re * h0_im + pa_im * h0_re)
    y = (jnp.einsum('tngp,ghp->ntgh', h_re, c_re.astype(f32))
         - jnp.einsum('tngp,ghp->ntgh', h_im, c_im.astype(f32)))
    y = y.reshape(n, t, D_INNER) + d_skip.astype(f32) * u
    return y, h_re[-1], h_im[-1]


def ssm_branch(xn, h0_re, h0_im, w_in, a_re, a_im, log_dt, b_re, b_im, c_re, c_im,
               d_skip, w_glu, b_glu, w_out):
    u, gate = jnp.split(xn @ w_in, 2, axis=-1)
    y, h_re, h_im = ssm_scan(u, h0_re, h0_im, a_re, a_im, log_dt, b_re, b_im,
                             c_re, c_im, d_skip)
    y = jax.nn.gelu(y, approximate=False)
    y = y * jax.nn.sigmoid(y @ w_glu.astype(jnp.float32) + b_glu.astype(jnp.float32))
    y = (y * jax.nn.silu(gate.astype(jnp.float32))).astype(xn.dtype)
    return y @ w_out, h_re, h_im


def conv_branch(xn, buf, w_in, dw_w, dw_b, ln_g, ln_b, w_out):
    a, b, gate = jnp.split(xn @ w_in, 3, axis=-1)
    v = a * jax.nn.sigmoid(b)
    full = jnp.concatenate([buf.astype(v.dtype), v], axis=1)
    z = lax.conv_general_dilated(
        full, dw_w.astype(v.dtype)[:, None, :], window_strides=(1,), padding='VALID',
        dimension_numbers=('NWC', 'WIO', 'NWC'), feature_group_count=D_INNER)
    z = z + dw_b.astype(v.dtype)
    z = layer_norm(z, ln_g, ln_b)
    z = jax.nn.silu(z) * jax.nn.silu(gate)
    return z @ w_out, full[:, -(CONV_WIDTH - 1):]


def trunk(x, h_re, h_im, conv_buf, p):
    n = x.shape[0]
    new_re, new_im, new_buf = [], [], []
    for i in range(DEPTH):
        j = i // N_MIXERS
        xn = rms_norm(x, p['norm_g'][i])
        if i % N_MIXERS == 0:
            h0r = None if h_re is None else h_re[j]
            h0i = None if h_im is None else h_im[j]
            out, hr, hi = ssm_branch(
                xn, h0r, h0i, p['ssm_w_in'][j], p['ssm_a_re'][j], p['ssm_a_im'][j],
                p['ssm_log_dt'][j], p['ssm_b_re'][j], p['ssm_b_im'][j], p['ssm_c_re'][j],
                p['ssm_c_im'][j], p['ssm_d'][j], p['ssm_w_glu'][j], p['ssm_b_glu'][j],
                p['ssm_w_out'][j])
            new_re.append(hr)
            new_im.append(hi)
        else:
            buf = (jnp.zeros((n, CONV_WIDTH - 1, D_INNER), x.dtype)
                   if conv_buf is None else conv_buf[j])
            out, nb = conv_branch(
                xn, buf, p['conv_w_in'][j], p['conv_dw_w'][j], p['conv_dw_b'][j],
                p['conv_ln_g'][j], p['conv_ln_b'][j], p['conv_w_out'][j])
            new_buf.append(nb)
        x = x + out.astype(x.dtype)
    y = rms_norm(x, p['final_norm_g'])
    return y, jnp.stack(new_re), jnp.stack(new_im), jnp.stack(new_buf)


def setup_inputs(seed: int = 0) -> dict:
    key = jax.random.key(seed)
    ks = jax.random.split(key, 26)
    f32 = jnp.float32
    nrm = lambda k, s, sc: sc * jax.random.normal(k, s, f32)
    n_idx = jnp.arange(SSM_STATE, dtype=f32)
    a_im0 = jnp.broadcast_to(jnp.pi * n_idx, (N_SSM_LAYERS, SSM_GROUPS, SSM_STATE))
    log_dt = (jax.random.uniform(ks[7], (N_SSM_LAYERS, SSM_GROUPS), f32)
              * (math.log(DT_MAX) - math.log(DT_MIN)) + math.log(DT_MIN))
    bsc = (2.0 * SSM_GROUP) ** -0.5
    csc = (2.0 * SSM_STATE) ** -0.5
    return {
        'x_prompt': nrm(ks[0], (BATCH, SEQ, D_MODEL), 1.0),
        'x_sample': nrm(ks[1], (DEC_BATCH, DEC_SEQ, D_MODEL), 1.0),
        'state_ssm_re': nrm(ks[2], (N_SSM_LAYERS, DEC_BATCH, SSM_GROUPS, SSM_STATE), 0.3),
        'state_ssm_im': nrm(ks[3], (N_SSM_LAYERS, DEC_BATCH, SSM_GROUPS, SSM_STATE), 0.3),
        'cache_conv': nrm(ks[4], (N_CONV_LAYERS, DEC_BATCH, CONV_WIDTH - 1, D_INNER), 0.5),
        'norm_g': 1.0 + nrm(ks[5], (DEPTH, D_MODEL), 0.02),
        'final_norm_g': 1.0 + nrm(ks[6], (D_MODEL,), 0.02),
        'ssm_w_in': nrm(ks[8], (N_SSM_LAYERS, D_MODEL, 2 * D_INNER), D_MODEL ** -0.5),
        'ssm_a_re': -0.5 + nrm(ks[9], (N_SSM_LAYERS, SSM_GROUPS, SSM_STATE), 0.01),
        'ssm_a_im': a_im0 + nrm(ks[10], (N_SSM_LAYERS, SSM_GROUPS, SSM_STATE), 0.01),
        'ssm_log_dt': log_dt,
        'ssm_b_re': nrm(ks[11], (N_SSM_LAYERS, SSM_GROUPS, SSM_STATE, SSM_GROUP), bsc),
        'ssm_b_im': nrm(ks[12], (N_SSM_LAYERS, SSM_GROUPS, SSM_STATE, SSM_GROUP), bsc),
        'ssm_c_re': nrm(ks[13], (N_SSM_LAYERS, SSM_GROUPS, SSM_GROUP, SSM_STATE), csc),
        'ssm_c_im': nrm(ks[14], (N_SSM_LAYERS, SSM_GROUPS, SSM_GROUP, SSM_STATE), csc),
        'ssm_d': 1.0 + nrm(ks[15], (N_SSM_LAYERS, D_INNER), 0.1),
        'ssm_w_glu': nrm(ks[16], (N_SSM_LAYERS, D_INNER, D_INNER), D_INNER ** -0.5),
        'ssm_b_glu': nrm(ks[17], (N_SSM_LAYERS, D_INNER), 0.01),
        'ssm_w_out': nrm(ks[18], (N_SSM_LAYERS, D_INNER, D_MODEL), D_INNER ** -0.5),
        'conv_w_in': nrm(ks[19], (N_CONV_LAYERS, D_MODEL, 3 * D_INNER), D_MODEL ** -0.5),
        'conv_dw_w': nrm(ks[20], (N_CONV_LAYERS, CONV_WIDTH, D_INNER), CONV_WIDTH ** -0.5),
        'conv_dw_b': nrm(ks[21], (N_CONV_LAYERS, D_INNER), 0.01),
        'conv_ln_g': 1.0 + nrm(ks[22], (N_CONV_LAYERS, D_INNER), 0.02),
        'conv_ln_b': nrm(ks[23], (N_CONV_LAYERS, D_INNER), 0.01),
        'conv_w_out': nrm(ks[24], (N_CONV_LAYERS, D_INNER, D_MODEL), D_INNER ** -0.5),
    }


def reference(x_prompt, x_sample, state_ssm_re, state_ssm_im, cache_conv, norm_g,
              final_norm_g, ssm_w_in, ssm_a_re, ssm_a_im, ssm_log_dt, ssm_b_re, ssm_b_im,
              ssm_c_re, ssm_c_im, ssm_d, ssm_w_glu, ssm_b_glu, ssm_w_out, conv_w_in,
              conv_dw_w, conv_dw_b, conv_ln_g, conv_ln_b, conv_w_out):
    p = dict(norm_g=norm_g, final_norm_g=final_norm_g, ssm_w_in=ssm_w_in,
             ssm_a_re=ssm_a_re, ssm_a_im=ssm_a_im, ssm_log_dt=ssm_log_dt,
             ssm_b_re=ssm_b_re, ssm_b_im=ssm_b_im, ssm_c_re=ssm_c_re, ssm_c_im=ssm_c_im,
             ssm_d=ssm_d, ssm_w_glu=ssm_w_glu, ssm_b_glu=ssm_b_glu, ssm_w_out=ssm_w_out,
             conv_w_in=conv_w_in, conv_dw_w=conv_dw_w, conv_dw_b=conv_dw_b,
             conv_ln_g=conv_ln_g, conv_ln_b=conv_ln_b, conv_w_out=conv_w_out)
    y_prompt, ssm_re_p, ssm_im_p, conv_p = trunk(x_prompt, None, None, None, p)
    y_sample, ssm_re_s, ssm_im_s, conv_s = trunk(x_sample, state_ssm_re, state_ssm_im,
                                                 cache_conv, p)
    return (y_prompt, y_sample, ssm_re_p, ssm_im_p, conv_p, ssm_re_s, ssm_im_s, conv_s)
```

```python
import functools
import math

import jax
import jax.numpy as jnp
from jax import lax
from jax.experimental import pallas as pl
from jax.experimental.pallas import tpu as pltpu

D_MODEL = 1024
D_INNER = 2048
SSM_GROUP = 16
SSM_GROUPS = D_INNER // SSM_GROUP
SSM_STATE = 64
CONV_WIDTH = 31
CONV_HIST = CONV_WIDTH - 1
EPS = 1e-6

F32 = jnp.float32
BF16 = jnp.bfloat16

LANES = 128
PROMPT_CHUNK = 16
GROUP_BLOCK = 8
HIST_ROWS = 32
VMEM_LIMIT = 56 * 1024 * 1024


def _cparams(sem):
    return pltpu.CompilerParams(dimension_semantics=sem, vmem_limit_bytes=VMEM_LIMIT)


def _rms_norm(x, g):
    ms = jnp.mean(x * x, axis=-1, keepdims=True)
    return x * lax.rsqrt(ms + EPS) * g


def _sigmoid(x):
    return 1.0 / (1.0 + jnp.exp(-x))


def _silu(x):
    return x * _sigmoid(x)


def _prep_kernel(are_c, aim_c, are_r, aim_r, ldt, bre_t, bim_t, cre, cim, cre_t, cim_t,
                 m_ref, w_ref, v_ref, dre_ref, dim_ref, *, chunk, width):
    kl = SSM_GROUP * chunk
    lane = lax.broadcasted_iota(jnp.int32, (1, width), 1)
    kexp = jnp.maximum(chunk - 1 - lax.shift_right_logical(lane, 4), 0).astype(F32)
    row = lax.broadcasted_iota(jnp.int32, (kl, 1), 0)
    kv = (lax.shift_right_logical(row, 4) + 1).astype(F32)
    lane8 = lax.broadcasted_iota(jnp.int32, (1, 8), 1)
    kdec = (chunk * lax.shift_left(jnp.ones_like(lane8), lane8)).astype(F32)
    for j in range(GROUP_BLOCK):
        dt = jnp.exp(ldt[j])
        ar = are_c[j]
        ai = aim_c[j]
        mag = jnp.exp(ar * dt)
        lre = mag * jnp.cos(ai * dt)
        lim = mag * jnp.sin(ai * dt)
        den = ar * ar + ai * ai
        fre = ((lre - 1.0) * ar + lim * ai) / den
        fim = (lim * ar - (lre - 1.0) * ai) / den
        bre = bre_t[j]
        bim = bim_t[j]
        bbre = fre * bre - fim * bim
        bbim = fre * bim + fim * bre
        pm = jnp.exp(kexp * (ar * dt))
        ang = kexp * (ai * dt)
        pre = pm * jnp.cos(ang)
        pim = pm * jnp.sin(ang)
        wre = pre * bbre - pim * bbim
        wim = pre * bbim + pim * bbre
        w_ref[j, 0:SSM_STATE, :] = wre[:, :kl].astype(BF16)
        w_ref[j, SSM_STATE:2 * SSM_STATE, :] = wim[:, :kl].astype(BF16)
        ktab = (jnp.dot(cre[j], wre, precision=lax.Precision.HIGHEST, preferred_element_type=F32)
                - jnp.dot(cim[j], wim, precision=lax.Precision.HIGHEST, preferred_element_type=F32))
        for t in range(chunk):
            shift = (width - SSM_GROUP * (chunk - 1 - t)) % width
            r = pltpu.roll(ktab, shift, 1) if shift else ktab
            blk = jnp.where(lane < SSM_GROUP * (t + 1), r, 0.0)
            m_ref[j, SSM_GROUP * t:SSM_GROUP * (t + 1), :] = blk[:, :kl].astype(BF16)
        arr = are_r[j]
        air = aim_r[j]
        vm = jnp.exp(kv * (arr * dt))
        va = kv * (air * dt)
        qre = vm * jnp.cos(va)
        qim = vm * jnp.sin(va)
        ct_re = cre_t[j]
        ct_im = cim_t[j]
        v_ref[j, :, 0:SSM_STATE] = (ct_re * qre - ct_im * qim).astype(BF16)
        v_ref[j, :, SSM_STATE:2 * SSM_STATE] = (-(ct_re * qim + ct_im * qre)).astype(BF16)
        dm = jnp.exp(kdec * (ar * dt))
        da = kdec * (ai * dt)
        dre_ref[j] = dm * jnp.cos(da)
        dim_ref[j] = dm * jnp.sin(da)


def _ssm_prep(a_re, a_im, log_dt, b_re, b_im, c_re, c_im, chunk):
    g, p, h = SSM_GROUPS, SSM_STATE, SSM_GROUP
    kl = h * chunk
    width = max(kl, LANES)
    gb = GROUP_BLOCK
    args = (a_re.reshape(g, p, 1), a_im.reshape(g, p, 1), a_re.reshape(g, 1, p), a_im.reshape(g, 1, p),
            log_dt.reshape(g, 1, 1),
            jnp.tile(b_re, (1, 1, width // h)), jnp.tile(b_im, (1, 1, width // h)),
            c_re, c_im, jnp.tile(c_re, (1, chunk, 1)), jnp.tile(c_im, (1, chunk, 1)))

    def spec(shape):
        return pl.BlockSpec((gb,) + shape, lambda i: (i, 0, 0))

    in_specs = [spec((p, 1)), spec((p, 1)), spec((1, p)), spec((1, p)), spec((1, 1)),
                spec((p, width)), spec((p, width)), spec((h, p)), spec((h, p)),
                spec((kl, p)), spec((kl, p))]
    out_shape = (jax.ShapeDtypeStruct((g, kl, kl), BF16), jax.ShapeDtypeStruct((g, 2 * p, kl), BF16),
                 jax.ShapeDtypeStruct((g, kl, 2 * p), BF16),
                 jax.ShapeDtypeStruct((g, p, 8), F32), jax.ShapeDtypeStruct((g, p, 8), F32))
    out_specs = (spec((kl, kl)), spec((2 * p, kl)), spec((kl, 2 * p)), spec((p, 8)), spec((p, 8)))
    return pl.pallas_call(
        functools.partial(_prep_kernel, chunk=chunk, width=width),
        grid=(g // gb,), in_specs=in_specs, out_specs=out_specs, out_shape=out_shape,
        compiler_params=_cparams(("arbitrary",)), name=f"s5_prep_{chunk}")(*args)


def _inproj0_kernel(x_ref, g_ref, w_ref, u_ref, gate_ref):
    xn = _rms_norm(x_ref[...], g_ref[...]).astype(BF16)
    res = lax.dot_general(w_ref[...], xn, (((1,), (1,)), ((), ())),
                          preferred_element_type=F32)
    u_ref[...] = res[:D_INNER].reshape(SSM_GROUPS, SSM_GROUP, -1).astype(BF16)
    gate_ref[...] = res[D_INNER:].astype(BF16)


def _inproj0(x2d, g, w_t, chunk, tc):
    nc = x2d.shape[0]
    return pl.pallas_call(
        _inproj0_kernel,
        grid=(nc // tc, chunk),
        in_specs=[pl.BlockSpec((tc, D_MODEL), lambda i, s: (i, s)),
                  pl.BlockSpec((1, D_MODEL), lambda i, s: (0, 0)),
                  pl.BlockSpec((2 * D_INNER, D_MODEL), lambda i, s: (0, 0))],
        out_specs=(pl.BlockSpec((SSM_GROUPS, SSM_GROUP, tc), lambda i, s: (0, s, i)),
                   pl.BlockSpec((None, D_INNER, tc), lambda i, s: (s, 0, i))),
        out_shape=(jax.ShapeDtypeStruct((SSM_GROUPS, SSM_GROUP * chunk, nc), BF16),
                   jax.ShapeDtypeStruct((chunk, D_INNER, nc), BF16)),
        compiler_params=_cparams(("arbitrary", "arbitrary")), name=f"l0_inproj_{chunk}")(x2d, g, w_t)


def _ssm_kernel(*refs, chunk, chunks_per_seq, carried):
    if carried:
        (u_ref, m_ref, w_ref, v_ref, dre_ref, dim_ref, d_ref, h0re_ref, h0im_ref,
         y_ref, hre_ref, him_ref) = refs
    else:
        u_ref, m_ref, w_ref, v_ref, dre_ref, dim_ref, d_ref, y_ref, hre_ref, him_ref = refs
    tc = u_ref.shape[-1]
    p = SSM_STATE
    lane = lax.broadcasted_iota(jnp.int32, (1, tc), 1)
    pos = lane & (chunks_per_seq - 1)
    for j in range(GROUP_BLOCK):
        u = u_ref[j]
        z = jnp.dot(w_ref[j], u, preferred_element_type=F32)
        sre, sim = z[:p], z[p:]
        dre, dim = dre_ref[j], dim_ref[j]
        if carried:
            pre_re, pre_im = h0re_ref[j], h0im_ref[j]
            ar, ai = dre[:, 0:1], dim[:, 0:1]
            hre_ref[j] = sre + ar * pre_re - ai * pre_im
            him_ref[j] = sim + ar * pre_im + ai * pre_re
        else:
            for k in range(chunks_per_seq.bit_length() - 1):
                d = 1 << k
                ar, ai = dre[:, k:k + 1], dim[:, k:k + 1]
                shr = pltpu.roll(sre, d, 1)
                shi = pltpu.roll(sim, d, 1)
                keep = pos >= d
                sre, sim = (sre + jnp.where(keep, ar * shr - ai * shi, 0.0),
                            sim + jnp.where(keep, ar * shi + ai * shr, 0.0))
            first = pos == 0
            pre_re = jnp.where(first, 0.0, pltpu.roll(sre, 1, 1))
            pre_im = jnp.where(first, 0.0, pltpu.roll(sim, 1, 1))
            nseq = tc // chunks_per_seq
            cols_re, cols_im = [], []
            for n in range(nseq):
                sel = lane == (n * chunks_per_seq + chunks_per_seq - 1)
                cols_re.append(jnp.sum(jnp.where(sel, sre, 0.0), axis=1, keepdims=True))
                cols_im.append(jnp.sum(jnp.where(sel, sim, 0.0), axis=1, keepdims=True))
            lane_n = lax.broadcasted_iota(jnp.int32, (1, nseq), 1)
            fre = jnp.zeros((p, nseq), F32)
            fim = jnp.zeros((p, nseq), F32)
            for n in range(nseq):
                fre = jnp.where(lane_n == n, cols_re[n], fre)
                fim = jnp.where(lane_n == n, cols_im[n], fim)
            hre_ref[j, 0] = fre
            him_ref[j, 0] = fim
        prev = jnp.concatenate([pre_re, pre_im], axis=0).astype(BF16)
        y = (jnp.dot(m_ref[j], u, preferred_element_type=F32)
             + jnp.dot(v_ref[j], prev, preferred_element_type=F32)
             + d_ref[j] * u.astype(F32))
        y_ref[:, SSM_GROUP * j:SSM_GROUP * (j + 1), :] = (
            y.reshape(chunk, SSM_GROUP, tc).astype(BF16))


def _ssm(u2, m, w, v, dre, dim, d_col, h0, chunk, chunks_per_seq, tc):
    g, kl, nc = u2.shape
    gb, p = GROUP_BLOCK, SSM_STATE
    carried = h0 is not None
    nt = nc // tc
    grid = (g // gb, nt)
    wspec = lambda shape: pl.BlockSpec((gb,) + shape, lambda a, i: (a, 0, 0))
    in_specs = [pl.BlockSpec((gb, kl, tc), lambda a, i: (a, 0, i)),
                wspec((kl, kl)), wspec((2 * p, kl)), wspec((kl, 2 * p)),
                wspec((p, 8)), wspec((p, 8)), wspec((kl, 1))]
    args = [u2, m, w, v, dre, dim, d_col]
    if carried:
        in_specs += [pl.BlockSpec((gb, p, tc), lambda a, i: (a, 0, i))] * 2
        args += list(h0)
        hshape = jax.ShapeDtypeStruct((g, p, nc), F32)
        hspec = pl.BlockSpec((gb, p, tc), lambda a, i: (a, 0, i))
    else:
        nseq = tc // chunks_per_seq
        hshape = jax.ShapeDtypeStruct((g, nt, p, nseq), F32)
        hspec = pl.BlockSpec((gb, 1, p, nseq), lambda a, i: (a, i, 0, 0))
    return pl.pallas_call(
        functools.partial(_ssm_kernel, chunk=chunk, chunks_per_seq=chunks_per_seq, carried=carried),
        grid=grid, in_specs=in_specs,
        out_specs=(pl.BlockSpec((chunk, SSM_GROUP * gb, tc), lambda a, i: (0, a, i)), hspec, hspec),
        out_shape=(jax.ShapeDtypeStruct((chunk, D_INNER, nc), BF16), hshape, hshape),
        compiler_params=_cparams(("arbitrary", "arbitrary")), name=f"s5_scan_{chunk}")(*args)


def _post0_kernel(y_ref, gate_ref, x_ref, wglu_ref, bglu_ref, wout_ref, o_ref):
    y = y_ref[...].astype(F32)
    y = 0.5 * y * (1.0 + lax.erf(y * math.sqrt(0.5)))
    z = jnp.dot(wglu_ref[...], y.astype(BF16), preferred_element_type=F32) + bglu_ref[...]
    y = y * _sigmoid(z)
    y = y * _silu(gate_ref[...].astype(F32))
    o = jnp.dot(wout_ref[...], y.astype(BF16), preferred_element_type=F32)
    o_ref[...] = x_ref[...] + o.T


def _post0(y2, gate2, x2d, wglu_t, bglu_col, wout_t, chunk, tc, phase_major_out):
    nc = x2d.shape[0]
    if phase_major_out:
        out_shape = jax.ShapeDtypeStruct((chunk, nc, D_MODEL), F32)
        out_spec = pl.BlockSpec((None, tc, D_MODEL), lambda i, s: (s, i, 0))
    else:
        out_shape = jax.ShapeDtypeStruct((nc, chunk * D_MODEL), F32)
        out_spec = pl.BlockSpec((tc, D_MODEL), lambda i, s: (i, s))
    return pl.pallas_call(
        _post0_kernel,
        grid=(nc // tc, chunk),
        in_specs=[pl.BlockSpec((None, D_INNER, tc), lambda i, s: (s, 0, i)),
                  pl.BlockSpec((None, D_INNER, tc), lambda i, s: (s, 0, i)),
                  pl.BlockSpec((tc, D_MODEL), lambda i, s: (i, s)),
                  pl.BlockSpec((D_INNER, D_INNER), lambda i, s: (0, 0)),
                  pl.BlockSpec((D_INNER, 1), lambda i, s: (0, 0)),
                  pl.BlockSpec((D_MODEL, D_INNER), lambda i, s: (0, 0))],
        out_specs=out_spec, out_shape=out_shape,
        compiler_params=_cparams(("arbitrary", "arbitrary")), name=f"l0_post_{chunk}")(
            y2, gate2, x2d, wglu_t, bglu_col, wout_t)


def _layer_norm_act(z, gate, lng, lnb):
    mu = jnp.mean(z, axis=-1, keepdims=True)
    zc = z - mu
    var = jnp.mean(zc * zc, axis=-1, keepdims=True)
    zn = zc * lax.rsqrt(var + EPS) * lng + lnb
    return _silu(zn) * _silu(gate)


CONV_ROWS = 64
CONV_COLS = 256


def _conv_prompt_kernel(x_ref, g_ref, w_ref, dww_ref, dwb_ref, lng_ref, lnb_ref,
                        z_ref, cst_ref, vs_ref, gate_ref, acc_ref, *, tm):
    j = pl.program_id(1)

    @pl.when(j == 0)
    def _():
        vs_ref[0:HIST_ROWS, :] = jnp.zeros((HIST_ROWS, D_INNER), F32)

    xn = _rms_norm(x_ref[...], g_ref[...]).astype(BF16)
    abg = jnp.dot(xn, w_ref[...], preferred_element_type=F32)
    v = abg[:, :D_INNER] * _sigmoid(abg[:, D_INNER:2 * D_INNER])
    vs_ref[HIST_ROWS:HIST_ROWS + tm, :] = v
    gate_ref[...] = abg[:, 2 * D_INNER:]

    def col_block(c, carry):
        cols = pl.ds(pl.multiple_of(c * CONV_COLS, CONV_COLS), CONV_COLS)
        for r0 in range(0, tm, CONV_ROWS):
            acc = jnp.broadcast_to(dwb_ref[:, cols], (CONV_ROWS, CONV_COLS))
            for k in range(CONV_WIDTH):
                off = r0 + HIST_ROWS - CONV_HIST + k
                acc = acc + dww_ref[k:k + 1, cols] * vs_ref[off:off + CONV_ROWS, cols]
            acc_ref[r0:r0 + CONV_ROWS, cols] = acc
        return carry

    lax.fori_loop(0, D_INNER // CONV_COLS, col_block, 0)
    z_ref[...] = _layer_norm_act(acc_ref[...], gate_ref[...], lng_ref[...], lnb_ref[...]).astype(BF16)
    cst_ref[...] = vs_ref[HIST_ROWS + tm - CONV_HIST:HIST_ROWS + tm, :]
    vs_ref[0:HIST_ROWS, :] = vs_ref[tm:tm + HIST_ROWS, :]


def _conv_prompt(x1, g, w_in, dw_w, dw_b, ln_g, ln_b, tm):
    n, t, _ = x1.shape
    const = lambda shape: pl.BlockSpec(shape, lambda a, j: (0,) * len(shape))
    return pl.pallas_call(
        functools.partial(_conv_prompt_kernel, tm=tm),
        grid=(n, t // tm),
        in_specs=[pl.BlockSpec((None, tm, D_MODEL), lambda a, j: (a, j, 0)),
                  const((1, D_MODEL)), const((D_MODEL, 3 * D_INNER)),
                  const((CONV_WIDTH, D_INNER)), const((1, D_INNER)),
                  const((1, D_INNER)), const((1, D_INNER))],
        out_specs=(pl.BlockSpec((None, tm, D_INNER), lambda a, j: (a, j, 0)),
                   pl.BlockSpec((None, CONV_HIST, D_INNER), lambda a, j: (a, 0, 0))),
        out_shape=(jax.ShapeDtypeStruct((n, t, D_INNER), BF16),
                   jax.ShapeDtypeStruct((n, CONV_HIST, D_INNER), F32)),
        scratch_shapes=[pltpu.VMEM((HIST_ROWS + tm, D_INNER), F32),
                        pltpu.VMEM((tm, D_INNER), F32),
                        pltpu.VMEM((tm, D_INNER), F32)],
        compiler_params=_cparams(("arbitrary", "arbitrary")), name="l1_conv_prompt")(
            x1, g, w_in, dw_w, dw_b, ln_g, ln_b)


def _inproj1_kernel(x_ref, g_ref, w_ref, v_ref, gate_ref):
    xn = _rms_norm(x_ref[...], g_ref[...]).astype(BF16)
    abg = jnp.dot(xn, w_ref[...], preferred_element_type=F32)
    v_ref[...] = abg[:, :D_INNER] * _sigmoid(abg[:, D_INNER:2 * D_INNER])
    gate_ref[...] = abg[:, 2 * D_INNER:]


def _inproj1(x1, g, w_in, tm):
    r = x1.shape[0]
    const = lambda shape: pl.BlockSpec(shape, lambda i: (0,) * len(shape))
    return pl.pallas_call(
        _inproj1_kernel, grid=(r // tm,),
        in_specs=[pl.BlockSpec((tm, D_MODEL), lambda i: (i, 0)), const((1, D_MODEL)),
                  const((D_MODEL, 3 * D_INNER))],
        out_specs=(pl.BlockSpec((tm, D_INNER), lambda i: (i, 0)),) * 2,
        out_shape=(jax.ShapeDtypeStruct((r, D_INNER), F32),) * 2,
        compiler_params=_cparams(("arbitrary",)), name="l1_inproj_sample")(x1, g, w_in)


SAMPLE_SEQ_TILE = 16
SAMPLE_COLS = 1024


def _conv_sample_kernel(cache_ref, v_ref, gate_ref, dww_ref, dwb_ref, lng_ref, lnb_ref,
                        z_ref, cst_ref, acc_ref, *, steps):
    ns = cache_ref.shape[0]
    for t in range(steps):
        for c0 in range(0, D_INNER, SAMPLE_COLS):
            acc = jnp.broadcast_to(dwb_ref[:, c0:c0 + SAMPLE_COLS], (ns, SAMPLE_COLS))
            for k in range(CONV_WIDTH):
                jrow = t + k
                if jrow < CONV_HIST:
                    src = cache_ref[:, jrow * D_INNER + c0:jrow * D_INNER + c0 + SAMPLE_COLS]
                else:
                    src = v_ref[jrow - CONV_HIST, :, c0:c0 + SAMPLE_COLS]
                acc = acc + dww_ref[k:k + 1, c0:c0 + SAMPLE_COLS] * src
            acc_ref[t, :, c0:c0 + SAMPLE_COLS] = acc
        z_ref[t] = _layer_norm_act(acc_ref[t], gate_ref[t], lng_ref[...], lnb_ref[...]).astype(BF16)
    keep = CONV_HIST - steps
    cst_ref[:, 0:keep * D_INNER] = cache_ref[:, steps * D_INNER:CONV_HIST * D_INNER]
    for t in range(steps):
        cst_ref[:, (keep + t) * D_INNER:(keep + t + 1) * D_INNER] = v_ref[t]


def _conv_sample(cache2d, v3, gate3, dw_w, dw_b, ln_g, ln_b):
    steps, n, _ = v3.shape
    ns = SAMPLE_SEQ_TILE
    const = lambda shape: pl.BlockSpec(shape, lambda i: (0,) * len(shape))
    return pl.pallas_call(
        functools.partial(_conv_sample_kernel, steps=steps), grid=(n // ns,),
        in_specs=[pl.BlockSpec((ns, CONV_HIST * D_INNER), lambda i: (i, 0)),
                  pl.BlockSpec((steps, ns, D_INNER), lambda i: (0, i, 0)),
                  pl.BlockSpec((steps, ns, D_INNER), lambda i: (0, i, 0)),
                  const((CONV_WIDTH, D_INNER)), const((1, D_INNER)), const((1, D_INNER)),
                  const((1, D_INNER))],
        out_specs=(pl.BlockSpec((steps, ns, D_INNER), lambda i: (0, i, 0)),
                   pl.BlockSpec((ns, CONV_HIST * D_INNER), lambda i: (i, 0))),
        out_shape=(jax.ShapeDtypeStruct((steps, n, D_INNER), BF16),
                   jax.ShapeDtypeStruct((n, CONV_HIST * D_INNER), F32)),
        scratch_shapes=[pltpu.VMEM((steps, ns, D_INNER), F32)],
        compiler_params=_cparams(("arbitrary",)), name="l1_conv_sample")(
            cache2d, v3, gate3, dw_w, dw_b, ln_g, ln_b)


def _out1_kernel(z_ref, x_ref, w_ref, g_ref, y_ref):
    x2 = x_ref[...] + jnp.dot(z_ref[...], w_ref[...], preferred_element_type=F32)
    y_ref[...] = _rms_norm(x2, g_ref[...])


def _out1(z, x1, w_out, g, tm):
    r = x1.shape[0]
    const = lambda shape: pl.BlockSpec(shape, lambda i: (0,) * len(shape))
    return pl.pallas_call(
        _out1_kernel, grid=(r // tm,),
        in_specs=[pl.BlockSpec((tm, D_INNER), lambda i: (i, 0)),
                  pl.BlockSpec((tm, D_MODEL), lambda i: (i, 0)),
                  const((D_INNER, D_MODEL)), const((1, D_MODEL))],
        out_specs=pl.BlockSpec((tm, D_MODEL), lambda i: (i, 0)),
        out_shape=jax.ShapeDtypeStruct((r, D_MODEL), F32),
        compiler_params=_cparams(("arbitrary",)), name="l1_out")(z, x1, w_out, g)


def kernel(x_prompt, x_sample, state_ssm_re, state_ssm_im, cache_conv, norm_g, final_norm_g, ssm_w_in, ssm_a_re, ssm_a_im, ssm_log_dt, ssm_b_re, ssm_b_im, ssm_c_re, ssm_c_im, ssm_d, ssm_w_glu, ssm_b_glu, ssm_w_out, conv_w_in, conv_dw_w, conv_dw_b, conv_ln_g, conv_ln_b, conv_w_out):
    n_p, t_p, _ = x_prompt.shape
    n_s, t_s, _ = x_sample.shape
    g, p = SSM_GROUPS, SSM_STATE

    w_in0_t = ssm_w_in[0].T.astype(BF16)
    w_glu_t = ssm_w_glu[0].T.astype(BF16)
    b_glu_col = ssm_b_glu[0].reshape(D_INNER, 1)
    w_out0_t = ssm_w_out[0].T.astype(BF16)
    w_in1 = conv_w_in[0].astype(BF16)
    w_out1 = conv_w_out[0].astype(BF16)
    g0 = norm_g[0].reshape(1, D_MODEL)
    g1 = norm_g[1].reshape(1, D_MODEL)
    gf = final_norm_g.reshape(1, D_MODEL)
    dw_w, dw_b = conv_dw_w[0], conv_dw_b[0].reshape(1, D_INNER)
    ln_g, ln_b = conv_ln_g[0].reshape(1, D_INNER), conv_ln_b[0].reshape(1, D_INNER)

    def layer0(x2d, chunk, chunks_per_seq, tc, h0, phase_major_out):
        m, w, v, dre, dim = _ssm_prep(ssm_a_re[0], ssm_a_im[0], ssm_log_dt[0], ssm_b_re[0], ssm_b_im[0],
                                      ssm_c_re[0], ssm_c_im[0], chunk)
        d_col = jnp.tile(ssm_d[0].reshape(g, 1, SSM_GROUP), (1, chunk, 1)).reshape(g, chunk * SSM_GROUP, 1)
        u2, gate2 = _inproj0(x2d, g0, w_in0_t, chunk, tc)
        y2, hre, him = _ssm(u2, m, w, v, dre, dim, d_col, h0, chunk, chunks_per_seq, tc)
        x1 = _post0(y2, gate2, x2d, w_glu_t, b_glu_col, w_out0_t, chunk, tc, phase_major_out)
        return x1, hre, him

    lp = PROMPT_CHUNK
    cps = t_p // lp
    tc_p = 2 * cps
    xp2d = x_prompt.reshape(n_p * cps, lp * D_MODEL)
    x1p, hre_p, him_p = layer0(xp2d, lp, cps, tc_p, None, False)
    ssm_re_p = jnp.transpose(hre_p, (1, 3, 0, 2)).reshape(1, n_p, g, p)
    ssm_im_p = jnp.transpose(him_p, (1, 3, 0, 2)).reshape(1, n_p, g, p)
    x1p = x1p.reshape(n_p, t_p, D_MODEL)
    zp, conv_p = _conv_prompt(x1p, g1, w_in1, dw_w, dw_b, ln_g, ln_b, tm=256)
    y_prompt = _out1(zp.reshape(n_p * t_p, D_INNER), x1p.reshape(n_p * t_p, D_MODEL), w_out1, gf, tm=512)
    y_prompt = y_prompt.reshape(n_p, t_p, D_MODEL)

    h0 = (jnp.transpose(state_ssm_re[0], (1, 2, 0)), jnp.transpose(state_ssm_im[0], (1, 2, 0)))
    xs2d = x_sample.reshape(n_s, t_s * D_MODEL)
    x1s, hre_s, him_s = layer0(xs2d, t_s, 1, n_s, h0, True)
    ssm_re_s = jnp.transpose(hre_s, (2, 0, 1))[None]
    ssm_im_s = jnp.transpose(him_s, (2, 0, 1))[None]
    x1s = x1s.reshape(t_s * n_s, D_MODEL)
    v_s, gate_s = _inproj1(x1s, g1, w_in1, tm=256)
    zs, conv_s = _conv_sample(cache_conv[0].reshape(n_s, CONV_HIST * D_INNER),
                              v_s.reshape(t_s, n_s, D_INNER), gate_s.reshape(t_s, n_s, D_INNER),
                              dw_w, dw_b, ln_g, ln_b)
    y_s = _out1(zs.reshape(t_s * n_s, D_INNER), x1s, w_out1, gf, tm=t_s * n_s)
    y_sample = jnp.transpose(y_s.reshape(t_s, n_s, D_MODEL), (1, 0, 2))

    return (y_prompt, y_sample, ssm_re_p, ssm_im_p, conv_p[None],
            ssm_re_s, ssm_im_s, conv_s.reshape(1, n_s, CONV_HIST, D_INNER))
```

```python
import functools
import math

import jax
import jax.numpy as jnp
from jax import lax
from jax.experimental import pallas as pl
from jax.experimental.pallas import tpu as pltpu

D_MODEL = 1024
D_INNER = 2048
SSM_GROUP = 16
SSM_GROUPS = D_INNER // SSM_GROUP
SSM_STATE = 64
CONV_WIDTH = 31
CONV_HIST = CONV_WIDTH - 1
EPS = 1e-6

F32 = jnp.float32
BF16 = jnp.bfloat16

LANES = 128
PROMPT_CHUNK = 16
GROUP_BLOCK = 8
HIST_ROWS = 32
VMEM_LIMIT = 56 * 1024 * 1024


def _cparams(sem):
    return pltpu.CompilerParams(dimension_semantics=sem, vmem_limit_bytes=VMEM_LIMIT)


def _rms_norm(x, g):
    ms = jnp.mean(x * x, axis=-1, keepdims=True)
    return x * lax.rsqrt(ms + EPS) * g


def _sigmoid(x):
    return 1.0 / (1.0 + jnp.exp(-x))


def _silu(x):
    return x * _sigmoid(x)


def _cpow(lre, lim, k, nbits, shape):
    pr = jnp.ones(shape, F32)
    pi = jnp.zeros(shape, F32)
    sr, si = lre, lim
    for b in range(nbits):
        take = (lax.shift_right_logical(k, b) & 1) == 1
        pr, pi = (jnp.where(take, pr * sr - pi * si, pr), jnp.where(take, pr * si + pi * sr, pi))
        if b + 1 < nbits:
            sr, si = sr * sr - si * si, 2.0 * sr * si
    return pr, pi


def _prep_kernel(are_ref, aim_ref, ldt_ref, bre_t, bim_t, cre, cim, cre_tt, cim_tt,
                 m_ref, w_ref, v_ref, dre_ref, dim_ref, *, chunk, width, n_pow):
    kl = SSM_GROUP * chunk
    lane = lax.broadcasted_iota(jnp.int32, (1, width), 1)
    step = lax.shift_right_logical(lane, 4)
    kexp = jnp.maximum(chunk - 1 - step, 0)
    kv = step + 1
    lane8 = lax.broadcasted_iota(jnp.int32, (1, 8), 1)
    dt_all = jnp.exp(ldt_ref[...])
    ar_all = are_ref[...]
    ai_all = aim_ref[...]
    mag = jnp.exp(ar_all * dt_all)
    lre_all = mag * jnp.cos(ai_all * dt_all)
    lim_all = mag * jnp.sin(ai_all * dt_all)
    den = ar_all * ar_all + ai_all * ai_all
    fre_all = ((lre_all - 1.0) * ar_all + lim_all * ai_all) / den
    fim_all = (lim_all * ar_all - (lre_all - 1.0) * ai_all) / den
    for j in range(GROUP_BLOCK):
        lre, lim = lre_all[:, j:j + 1], lim_all[:, j:j + 1]
        fre, fim = fre_all[:, j:j + 1], fim_all[:, j:j + 1]
        bre = bre_t[j]
        bim = bim_t[j]
        bbre = fre * bre - fim * bim
        bbim = fre * bim + fim * bre
        pre, pim = _cpow(lre, lim, kexp, (chunk - 1).bit_length(), (SSM_STATE, width))
        wre = pre * bbre - pim * bbim
        wim = pre * bbim + pim * bbre
        w_ref[j, 0:SSM_STATE, :] = wre[:, :kl].astype(BF16)
        w_ref[j, SSM_STATE:2 * SSM_STATE, :] = wim[:, :kl].astype(BF16)
        ktab = (jnp.dot(cre[j], wre, precision=lax.Precision.HIGHEST, preferred_element_type=F32)
                - jnp.dot(cim[j], wim, precision=lax.Precision.HIGHEST, preferred_element_type=F32))
        for t in range(chunk):
            shift = (width - SSM_GROUP * (chunk - 1 - t)) % width
            r = pltpu.roll(ktab, shift, 1) if shift else ktab
            blk = jnp.where(lane < SSM_GROUP * (t + 1), r, 0.0)
            m_ref[j, SSM_GROUP * t:SSM_GROUP * (t + 1), :] = blk[:, :kl].astype(BF16)
        qre, qim = _cpow(lre, lim, kv, (width // SSM_GROUP).bit_length(), (SSM_STATE, width))
        ct_re = cre_tt[j]
        ct_im = cim_tt[j]
        v_t = jnp.concatenate([ct_re * qre - ct_im * qim, -(ct_re * qim + ct_im * qre)], axis=0)
        v_ref[j] = v_t.T[:kl, :].astype(BF16)
        sr, si = lre, lim
        for _ in range(chunk.bit_length() - 1):
            sr, si = sr * sr - si * si, 2.0 * sr * si
        dre = jnp.zeros((SSM_STATE, 8), F32)
        dim = jnp.zeros((SSM_STATE, 8), F32)
        for k in range(n_pow):
            dre = jnp.where(lane8 == k, sr, dre)
            dim = jnp.where(lane8 == k, si, dim)
            if k + 1 < n_pow:
                sr, si = sr * sr - si * si, 2.0 * sr * si
        dre_ref[j] = dre
        dim_ref[j] = dim


def _ssm_prep(a_re, a_im, log_dt, b_re, b_im, c_re, c_im, chunk, n_pow):
    g, p, h = SSM_GROUPS, SSM_STATE, SSM_GROUP
    kl = h * chunk
    width = max(kl, LANES)
    gb = GROUP_BLOCK
    reps = width // h
    cols = lambda a: jnp.transpose(a.reshape(g // gb, gb, p), (0, 2, 1))
    tile_t = lambda c: jnp.tile(jnp.swapaxes(c, 1, 2), (1, 1, reps))
    args = (cols(a_re), cols(a_im), log_dt.reshape(g // gb, 1, gb),
            jnp.tile(b_re, (1, 1, reps)), jnp.tile(b_im, (1, 1, reps)),
            c_re, c_im, tile_t(c_re), tile_t(c_im))

    def spec(shape):
        return pl.BlockSpec((gb,) + shape, lambda i: (i, 0, 0))

    def blockwise(shape):
        return pl.BlockSpec((None,) + shape, lambda i: (i, 0, 0))

    in_specs = [blockwise((p, gb)), blockwise((p, gb)), blockwise((1, gb)),
                spec((p, width)), spec((p, width)), spec((h, p)), spec((h, p)),
                spec((p, width)), spec((p, width))]
    out_shape = (jax.ShapeDtypeStruct((g, kl, kl), BF16), jax.ShapeDtypeStruct((g, 2 * p, kl), BF16),
                 jax.ShapeDtypeStruct((g, kl, 2 * p), BF16),
                 jax.ShapeDtypeStruct((g, p, 8), F32), jax.ShapeDtypeStruct((g, p, 8), F32))
    out_specs = (spec((kl, kl)), spec((2 * p, kl)), spec((kl, 2 * p)), spec((p, 8)), spec((p, 8)))
    return pl.pallas_call(
        functools.partial(_prep_kernel, chunk=chunk, width=width, n_pow=n_pow),
        grid=(g // gb,), in_specs=in_specs, out_specs=out_specs, out_shape=out_shape,
        compiler_params=_cparams(("arbitrary",)), name=f"s5_prep_{chunk}")(*args)


def _inproj0_kernel(x_ref, g_ref, w_ref, u_ref, gate_ref):
    xn = _rms_norm(x_ref[...], g_ref[...]).astype(BF16)
    res = lax.dot_general(w_ref[...], xn, (((1,), (1,)), ((), ())),
                          preferred_element_type=F32)
    u_ref[...] = res[:D_INNER].reshape(SSM_GROUPS, SSM_GROUP, -1).astype(BF16)
    gate_ref[...] = res[D_INNER:].astype(BF16)


def _inproj0(x2d, g, w_t, chunk, tc):
    nc = x2d.shape[0]
    return pl.pallas_call(
        _inproj0_kernel,
        grid=(nc // tc, chunk),
        in_specs=[pl.BlockSpec((tc, D_MODEL), lambda i, s: (i, s)),
                  pl.BlockSpec((1, D_MODEL), lambda i, s: (0, 0)),
                  pl.BlockSpec((2 * D_INNER, D_MODEL), lambda i, s: (0, 0))],
        out_specs=(pl.BlockSpec((SSM_GROUPS, SSM_GROUP, tc), lambda i, s: (0, s, i)),
                   pl.BlockSpec((None, D_INNER, tc), lambda i, s: (s, 0, i))),
        out_shape=(jax.ShapeDtypeStruct((SSM_GROUPS, SSM_GROUP * chunk, nc), BF16),
                   jax.ShapeDtypeStruct((chunk, D_INNER, nc), BF16)),
        compiler_params=_cparams(("arbitrary", "arbitrary")), name=f"l0_inproj_{chunk}")(x2d, g, w_t)


def _ssm_kernel(*refs, chunk, chunks_per_seq, carried):
    if carried:
        (u_ref, m_ref, w_ref, v_ref, dre_ref, dim_ref, d_ref, h0re_ref, h0im_ref,
         y_ref, hre_ref, him_ref) = refs[:-2]
    else:
        u_ref, m_ref, w_ref, v_ref, dre_ref, dim_ref, d_ref, y_ref, hre_ref, him_ref = refs[:-2]
    sre_ref, sim_ref = refs[-2:]
    tc = u_ref.shape[-1]
    p = SSM_STATE
    lane = lax.broadcasted_iota(jnp.int32, (1, tc), 1)
    pos = lane & (chunks_per_seq - 1)
    groups = range(GROUP_BLOCK)
    for j in groups:
        z = jnp.dot(w_ref[j], u_ref[j], preferred_element_type=F32)
        sre_ref[j] = z[:p]
        sim_ref[j] = z[p:]
    if carried:
        for j in groups:
            pre_re, pre_im = h0re_ref[j], h0im_ref[j]
            ar, ai = dre_ref[j][:, 0:1], dim_ref[j][:, 0:1]
            hre_ref[j] = sre_ref[j] + ar * pre_re - ai * pre_im
            him_ref[j] = sim_ref[j] + ar * pre_im + ai * pre_re
    else:
        for k in range(chunks_per_seq.bit_length() - 1):
            d = 1 << k
            keep = pos >= d
            for j in groups:
                sre, sim = sre_ref[j], sim_ref[j]
                ar, ai = dre_ref[j][:, k:k + 1], dim_ref[j][:, k:k + 1]
                shr = pltpu.roll(sre, d, 1)
                shi = pltpu.roll(sim, d, 1)
                sre_ref[j] = sre + jnp.where(keep, ar * shr - ai * shi, 0.0)
                sim_ref[j] = sim + jnp.where(keep, ar * shi + ai * shr, 0.0)
        nseq = tc // chunks_per_seq
        lane_n = lax.broadcasted_iota(jnp.int32, (1, nseq), 1)
        for j in groups:
            sre, sim = sre_ref[j], sim_ref[j]
            fre = jnp.zeros((p, nseq), F32)
            fim = jnp.zeros((p, nseq), F32)
            for n in range(nseq):
                sel = lane == (n * chunks_per_seq + chunks_per_seq - 1)
                fre = jnp.where(lane_n == n, jnp.sum(jnp.where(sel, sre, 0.0), axis=1, keepdims=True), fre)
                fim = jnp.where(lane_n == n, jnp.sum(jnp.where(sel, sim, 0.0), axis=1, keepdims=True), fim)
            hre_ref[j, 0] = fre
            him_ref[j, 0] = fim
    first = pos == 0
    for j in groups:
        u = u_ref[j]
        if carried:
            pre_re, pre_im = h0re_ref[j], h0im_ref[j]
        else:
            pre_re = jnp.where(first, 0.0, pltpu.roll(sre_ref[j], 1, 1))
            pre_im = jnp.where(first, 0.0, pltpu.roll(sim_ref[j], 1, 1))
        prev = jnp.concatenate([pre_re, pre_im], axis=0).astype(BF16)
        y = (jnp.dot(m_ref[j], u, preferred_element_type=F32)
             + jnp.dot(v_ref[j], prev, preferred_element_type=F32)
             + d_ref[j] * u.astype(F32))
        y_ref[:, SSM_GROUP * j:SSM_GROUP * (j + 1), :] = (
            y.reshape(chunk, SSM_GROUP, tc).astype(BF16))


def _ssm(u2, m, w, v, dre, dim, d_col, h0, chunk, chunks_per_seq, tc):
    g, kl, nc = u2.shape
    gb, p = GROUP_BLOCK, SSM_STATE
    carried = h0 is not None
    nt = nc // tc
    grid = (g // gb, nt)
    wspec = lambda shape: pl.BlockSpec((gb,) + shape, lambda a, i: (a, 0, 0))
    in_specs = [pl.BlockSpec((gb, kl, tc), lambda a, i: (a, 0, i)),
                wspec((kl, kl)), wspec((2 * p, kl)), wspec((kl, 2 * p)),
                wspec((p, 8)), wspec((p, 8)), wspec((kl, 1))]
    args = [u2, m, w, v, dre, dim, d_col]
    if carried:
        in_specs += [pl.BlockSpec((gb, p, tc), lambda a, i: (a, 0, i))] * 2
        args += list(h0)
        hshape = jax.ShapeDtypeStruct((g, p, nc), F32)
        hspec = pl.BlockSpec((gb, p, tc), lambda a, i: (a, 0, i))
    else:
        nseq = tc // chunks_per_seq
        hshape = jax.ShapeDtypeStruct((g, nt, p, nseq), F32)
        hspec = pl.BlockSpec((gb, 1, p, nseq), lambda a, i: (a, i, 0, 0))
    return pl.pallas_call(
        functools.partial(_ssm_kernel, chunk=chunk, chunks_per_seq=chunks_per_seq, carried=carried),
        grid=grid, in_specs=in_specs,
        out_specs=(pl.BlockSpec((chunk, SSM_GROUP * gb, tc), lambda a, i: (0, a, i)), hspec, hspec),
        out_shape=(jax.ShapeDtypeStruct((chunk, D_INNER, nc), BF16), hshape, hshape),
        scratch_shapes=[pltpu.VMEM((gb, p, tc), F32), pltpu.VMEM((gb, p, tc), F32)],
        compiler_params=_cparams(("arbitrary", "arbitrary")), name=f"s5_scan_{chunk}")(*args)


def _post0_kernel(y_ref, gate_ref, x_ref, wglu_ref, bglu_ref, wout_ref, o_ref):
    y = y_ref[...].astype(F32)
    y = 0.5 * y * (1.0 + lax.erf(y * math.sqrt(0.5)))
    z = jnp.dot(wglu_ref[...], y.astype(BF16), preferred_element_type=F32) + bglu_ref[...]
    y = y * _sigmoid(z)
    y = y * _silu(gate_ref[...].astype(F32))
    o = jnp.dot(wout_ref[...], y.astype(BF16), preferred_element_type=F32)
    o_ref[...] = x_ref[...] + o.T


def _post0(y2, gate2, x2d, wglu_t, bglu_col, wout_t, chunk, tc, phase_major_out):
    nc = x2d.shape[0]
    if phase_major_out:
        out_shape = jax.ShapeDtypeStruct((chunk, nc, D_MODEL), F32)
        out_spec = pl.BlockSpec((None, tc, D_MODEL), lambda i, s: (s, i, 0))
    else:
        out_shape = jax.ShapeDtypeStruct((nc, chunk * D_MODEL), F32)
        out_spec = pl.BlockSpec((tc, D_MODEL), lambda i, s: (i, s))
    return pl.pallas_call(
        _post0_kernel,
        grid=(nc // tc, chunk),
        in_specs=[pl.BlockSpec((None, D_INNER, tc), lambda i, s: (s, 0, i)),
                  pl.BlockSpec((None, D_INNER, tc), lambda i, s: (s, 0, i)),
                  pl.BlockSpec((tc, D_MODEL), lambda i, s: (i, s)),
                  pl.BlockSpec((D_INNER, D_INNER), lambda i, s: (0, 0)),
                  pl.BlockSpec((D_INNER, 1), lambda i, s: (0, 0)),
                  pl.BlockSpec((D_MODEL, D_INNER), lambda i, s: (0, 0))],
        out_specs=out_spec, out_shape=out_shape,
        compiler_params=_cparams(("arbitrary", "arbitrary")), name=f"l0_post_{chunk}")(
            y2, gate2, x2d, wglu_t, bglu_col, wout_t)


def _layer_norm_act(z, gate, lng, lnb):
    mu = jnp.mean(z, axis=-1, keepdims=True)
    zc = z - mu
    var = jnp.mean(zc * zc, axis=-1, keepdims=True)
    zn = zc * lax.rsqrt(var + EPS) * lng + lnb
    return _silu(zn) * _silu(gate)


CONV_ROWS = 64
CONV_COLS = 256


def _conv_prompt_kernel(x_ref, g_ref, w_ref, dww_ref, dwb_ref, lng_ref, lnb_ref,
                        z_ref, cst_ref, vs_ref, gate_ref, acc_ref, *, tm):
    j = pl.program_id(1)

    @pl.when(j == 0)
    def _():
        vs_ref[0:HIST_ROWS, :] = jnp.zeros((HIST_ROWS, D_INNER), F32)

    xn = _rms_norm(x_ref[...], g_ref[...]).astype(BF16)
    abg = jnp.dot(xn, w_ref[...], preferred_element_type=F32)
    v = abg[:, :D_INNER] * _sigmoid(abg[:, D_INNER:2 * D_INNER])
    vs_ref[HIST_ROWS:HIST_ROWS + tm, :] = v
    gate_ref[...] = abg[:, 2 * D_INNER:]

    def col_block(c, carry):
        cols = pl.ds(pl.multiple_of(c * CONV_COLS, CONV_COLS), CONV_COLS)
        for r0 in range(0, tm, CONV_ROWS):
            acc = jnp.broadcast_to(dwb_ref[:, cols], (CONV_ROWS, CONV_COLS))
            for k in range(CONV_WIDTH):
                off = r0 + HIST_ROWS - CONV_HIST + k
                acc = acc + dww_ref[k:k + 1, cols] * vs_ref[off:off + CONV_ROWS, cols]
            acc_ref[r0:r0 + CONV_ROWS, cols] = acc
        return carry

    lax.fori_loop(0, D_INNER // CONV_COLS, col_block, 0)
    z_ref[...] = _layer_norm_act(acc_ref[...], gate_ref[...], lng_ref[...], lnb_ref[...]).astype(BF16)
    cst_ref[...] = vs_ref[HIST_ROWS + tm - CONV_HIST:HIST_ROWS + tm, :]
    vs_ref[0:HIST_ROWS, :] = vs_ref[tm:tm + HIST_ROWS, :]


def _conv_prompt(x1, g, w_in, dw_w, dw_b, ln_g, ln_b, tm):
    n, t, _ = x1.shape
    const = lambda shape: pl.BlockSpec(shape, lambda a, j: (0,) * len(shape))
    return pl.pallas_call(
        functools.partial(_conv_prompt_kernel, tm=tm),
        grid=(n, t // tm),
        in_specs=[pl.BlockSpec((None, tm, D_MODEL), lambda a, j: (a, j, 0)),
                  const((1, D_MODEL)), const((D_MODEL, 3 * D_INNER)),
                  const((CONV_WIDTH, D_INNER)), const((1, D_INNER)),
                  const((1, D_INNER)), const((1, D_INNER))],
        out_specs=(pl.BlockSpec((None, tm, D_INNER), lambda a, j: (a, j, 0)),
                   pl.BlockSpec((None, CONV_HIST, D_INNER), lambda a, j: (a, 0, 0))),
        out_shape=(jax.ShapeDtypeStruct((n, t, D_INNER), BF16),
                   jax.ShapeDtypeStruct((n, CONV_HIST, D_INNER), F32)),
        scratch_shapes=[pltpu.VMEM((HIST_ROWS + tm, D_INNER), F32),
                        pltpu.VMEM((tm, D_INNER), F32),
                        pltpu.VMEM((tm, D_INNER), F32)],
        compiler_params=_cparams(("arbitrary", "arbitrary")), name="l1_conv_prompt")(
            x1, g, w_in, dw_w, dw_b, ln_g, ln_b)


def _inproj1_kernel(x_ref, g_ref, w_ref, v_ref, gate_ref):
    xn = _rms_norm(x_ref[...], g_ref[...]).astype(BF16)
    abg = jnp.dot(xn, w_ref[...], preferred_element_type=F32)
    v_ref[...] = abg[:, :D_INNER] * _sigmoid(abg[:, D_INNER:2 * D_INNER])
    gate_ref[...] = abg[:, 2 * D_INNER:]


def _inproj1(x1, g, w_in, tm):
    r = x1.shape[0]
    const = lambda shape: pl.BlockSpec(shape, lambda i: (0,) * len(shape))
    return pl.pallas_call(
        _inproj1_kernel, grid=(r // tm,),
        in_specs=[pl.BlockSpec((tm, D_MODEL), lambda i: (i, 0)), const((1, D_MODEL)),
                  const((D_MODEL, 3 * D_INNER))],
        out_specs=(pl.BlockSpec((tm, D_INNER), lambda i: (i, 0)),) * 2,
        out_shape=(jax.ShapeDtypeStruct((r, D_INNER), F32),) * 2,
        compiler_params=_cparams(("arbitrary",)), name="l1_inproj_sample")(x1, g, w_in)


SAMPLE_SEQ_TILE = 16
SAMPLE_COLS = 1024


def _conv_sample_kernel(cache_ref, v_ref, gate_ref, dww_ref, dwb_ref, lng_ref, lnb_ref,
                        z_ref, cst_ref, acc_ref, *, steps):
    ns = cache_ref.shape[0]
    for t in range(steps):
        for c0 in range(0, D_INNER, SAMPLE_COLS):
            acc = jnp.broadcast_to(dwb_ref[:, c0:c0 + SAMPLE_COLS], (ns, SAMPLE_COLS))
            for k in range(CONV_WIDTH):
                jrow = t + k
                if jrow < CONV_HIST:
                    src = cache_ref[:, jrow * D_INNER + c0:jrow * D_INNER + c0 + SAMPLE_COLS]
                else:
                    src = v_ref[jrow - CONV_HIST, :, c0:c0 + SAMPLE_COLS]
                acc = acc + dww_ref[k:k + 1, c0:c0 + SAMPLE_COLS] * src
            acc_ref[t, :, c0:c0 + SAMPLE_COLS] = acc
        z_ref[t] = _layer_norm_act(acc_ref[t], gate_ref[t], lng_ref[...], lnb_ref[...]).astype(BF16)
    keep = CONV_HIST - steps
    cst_ref[:, 0:keep * D_INNER] = cache_ref[:, steps * D_INNER:CONV_HIST * D_INNER]
    for t in range(steps):
        cst_ref[:, (keep + t) * D_INNER:(keep + t + 1) * D_INNER] = v_ref[t]


def _conv_sample(cache2d, v3, gate3, dw_w, dw_b, ln_g, ln_b):
    steps, n, _ = v3.shape
    ns = SAMPLE_SEQ_TILE
    const = lambda shape: pl.BlockSpec(shape, lambda i: (0,) * len(shape))
    return pl.pallas_call(
        functools.partial(_conv_sample_kernel, steps=steps), grid=(n // ns,),
        in_specs=[pl.BlockSpec((ns, CONV_HIST * D_INNER), lambda i: (i, 0)),
                  pl.BlockSpec((steps, ns, D_INNER), lambda i: (0, i, 0)),
                  pl.BlockSpec((steps, ns, D_INNER), lambda i: (0, i, 0)),
                  const((CONV_WIDTH, D_INNER)), const((1, D_INNER)), const((1, D_INNER)),
                  const((1, D_INNER))],
        out_specs=(pl.BlockSpec((steps, ns, D_INNER), lambda i: (0, i, 0)),
                   pl.BlockSpec((ns, CONV_HIST * D_INNER), lambda i: (i, 0))),
        out_shape=(jax.ShapeDtypeStruct((steps, n, D_INNER), BF16),
                   jax.ShapeDtypeStruct((n, CONV_HIST * D_INNER), F32)),
        scratch_shapes=[pltpu.VMEM((steps, ns, D_INNER), F32)],
        compiler_params=_cparams(("arbitrary",)), name="l1_conv_sample")(
            cache2d, v3, gate3, dw_w, dw_b, ln_g, ln_b)


def _out1_kernel(z_ref, x_ref, w_ref, g_ref, y_ref):
    x2 = x_ref[...] + jnp.dot(z_ref[...], w_ref[...], preferred_element_type=F32)
    y_ref[...] = _rms_norm(x2, g_ref[...])


def _out1(z, x1, w_out, g, tm):
    r = x1.shape[0]
    const = lambda shape: pl.BlockSpec(shape, lambda i: (0,) * len(shape))
    return pl.pallas_call(
        _out1_kernel, grid=(r // tm,),
        in_specs=[pl.BlockSpec((tm, D_INNER), lambda i: (i, 0)),
                  pl.BlockSpec((tm, D_MODEL), lambda i: (i, 0)),
                  const((D_INNER, D_MODEL)), const((1, D_MODEL))],
        out_specs=pl.BlockSpec((tm, D_MODEL), lambda i: (i, 0)),
        out_shape=jax.ShapeDtypeStruct((r, D_MODEL), F32),
        compiler_params=_cparams(("arbitrary",)), name="l1_out")(z, x1, w_out, g)


def kernel(x_prompt, x_sample, state_ssm_re, state_ssm_im, cache_conv, norm_g, final_norm_g, ssm_w_in, ssm_a_re, ssm_a_im, ssm_log_dt, ssm_b_re, ssm_b_im, ssm_c_re, ssm_c_im, ssm_d, ssm_w_glu, ssm_b_glu, ssm_w_out, conv_w_in, conv_dw_w, conv_dw_b, conv_ln_g, conv_ln_b, conv_w_out):
    n_p, t_p, _ = x_prompt.shape
    n_s, t_s, _ = x_sample.shape
    g, p = SSM_GROUPS, SSM_STATE

    w_in0_t = ssm_w_in[0].T.astype(BF16)
    w_glu_t = ssm_w_glu[0].T.astype(BF16)
    b_glu_col = ssm_b_glu[0].reshape(D_INNER, 1)
    w_out0_t = ssm_w_out[0].T.astype(BF16)
    w_in1 = conv_w_in[0].astype(BF16)
    w_out1 = conv_w_out[0].astype(BF16)
    g0 = norm_g[0].reshape(1, D_MODEL)
    g1 = norm_g[1].reshape(1, D_MODEL)
    gf = final_norm_g.reshape(1, D_MODEL)
    dw_w, dw_b = conv_dw_w[0], conv_dw_b[0].reshape(1, D_INNER)
    ln_g, ln_b = conv_ln_g[0].reshape(1, D_INNER), conv_ln_b[0].reshape(1, D_INNER)

    def layer0(x2d, chunk, chunks_per_seq, tc, h0, phase_major_out):
        m, w, v, dre, dim = _ssm_prep(ssm_a_re[0], ssm_a_im[0], ssm_log_dt[0], ssm_b_re[0], ssm_b_im[0],
                                      ssm_c_re[0], ssm_c_im[0], chunk, max(1, chunks_per_seq.bit_length() - 1))
        d_col = jnp.tile(ssm_d[0].reshape(g, 1, SSM_GROUP), (1, chunk, 1)).reshape(g, chunk * SSM_GROUP, 1)
        u2, gate2 = _inproj0(x2d, g0, w_in0_t, chunk, tc)
        y2, hre, him = _ssm(u2, m, w, v, dre, dim, d_col, h0, chunk, chunks_per_seq, tc)
        x1 = _post0(y2, gate2, x2d, w_glu_t, b_glu_col, w_out0_t, chunk, tc, phase_major_out)
        return x1, hre, him

    lp = PROMPT_CHUNK
    cps = t_p // lp
    tc_p = 2 * cps
    xp2d = x_prompt.reshape(n_p * cps, lp * D_MODEL)
    x1p, hre_p, him_p = layer0(xp2d, lp, cps, tc_p, None, False)
    ssm_re_p = jnp.transpose(hre_p, (1, 3, 0, 2)).reshape(1, n_p, g, p)
    ssm_im_p = jnp.transpose(him_p, (1, 3, 0, 2)).reshape(1, n_p, g, p)
    x1p = x1p.reshape(n_p, t_p, D_MODEL)
    zp, conv_p = _conv_prompt(x1p, g1, w_in1, dw_w, dw_b, ln_g, ln_b, tm=256)
    y_prompt = _out1(zp.reshape(n_p * t_p, D_INNER), x1p.reshape(n_p * t_p, D_MODEL), w_out1, gf, tm=512)
    y_prompt = y_prompt.reshape(n_p, t_p, D_MODEL)

    h0 = (jnp.transpose(state_ssm_re[0], (1, 2, 0)), jnp.transpose(state_ssm_im[0], (1, 2, 0)))
    xs2d = x_sample.reshape(n_s, t_s * D_MODEL)
    x1s, hre_s, him_s = layer0(xs2d, t_s, 1, n_s, h0, True)
    ssm_re_s = jnp.transpose(hre_s, (2, 0, 1))[None]
    ssm_im_s = jnp.transpose(him_s, (2, 0, 1))[None]
    x1s = x1s.reshape(t_s * n_s, D_MODEL)
    v_s, gate_s = _inproj1(x1s, g1, w_in1, tm=256)
    zs, conv_s = _conv_sample(cache_conv[0].reshape(n_s, CONV_HIST * D_INNER),
                              v_s.reshape(t_s, n_s, D_INNER), gate_s.reshape(t_s, n_s, D_INNER),
                              dw_w, dw_b, ln_g, ln_b)
    y_s = _out1(zs.reshape(t_s * n_s, D_INNER), x1s, w_out1, gf, tm=t_s * n_s)
    y_sample = jnp.transpose(y_s.reshape(t_s, n_s, D_MODEL), (1, 0, 2))

    return (y_prompt, y_sample, ssm_re_p, ssm_im_p, conv_p[None],
            ssm_re_s, ssm_im_s, conv_s.reshape(1, n_s, CONV_HIST, D_INNER))
```

```python
import functools
import math

import jax
import jax.numpy as jnp
from jax import lax
from jax.experimental import pallas as pl
from jax.experimental.pallas import tpu as pltpu

D_MODEL = 1024
D_INNER = 2048
SSM_GROUP = 16
SSM_GROUPS = D_INNER // SSM_GROUP
SSM_STATE = 64
CONV_WIDTH = 31
CONV_HIST = CONV_WIDTH - 1
EPS = 1e-6

F32 = jnp.float32
BF16 = jnp.bfloat16

LANES = 128
PROMPT_CHUNK = 16
GROUP_BLOCK = 8
L1_CHUNK_ROWS = 32
VMEM_LIMIT = 56 * 1024 * 1024


def _cparams(sem):
    return pltpu.CompilerParams(dimension_semantics=sem, vmem_limit_bytes=VMEM_LIMIT)


def _rms_norm(x, g):
    ms = jnp.mean(x * x, axis=-1, keepdims=True)
    return x * lax.rsqrt(ms + EPS) * g


def _sigmoid(x):
    return 1.0 / (1.0 + jnp.exp(-x))


def _silu(x):
    return x * _sigmoid(x)


def _cpow(lre, lim, k, nbits, shape):
    pr = jnp.ones(shape, F32)
    pi = jnp.zeros(shape, F32)
    sr, si = lre, lim
    for b in range(nbits):
        take = (lax.shift_right_logical(k, b) & 1) == 1
        pr, pi = (jnp.where(take, pr * sr - pi * si, pr), jnp.where(take, pr * si + pi * sr, pi))
        if b + 1 < nbits:
            sr, si = sr * sr - si * si, 2.0 * sr * si
    return pr, pi


def _prep_kernel(are_ref, aim_ref, ldt_ref, bre_t, bim_t, cre, cim, cre_tt, cim_tt,
                 m_ref, w_ref, v_ref, dre_ref, dim_ref, *, chunk, width, n_pow):
    kl = SSM_GROUP * chunk
    lane = lax.broadcasted_iota(jnp.int32, (1, width), 1)
    step = lax.shift_right_logical(lane, 4)
    kexp = jnp.maximum(chunk - 1 - step, 0)
    kv = step + 1
    lane8 = lax.broadcasted_iota(jnp.int32, (1, 8), 1)
    dt_all = jnp.exp(ldt_ref[...])
    ar_all = are_ref[...]
    ai_all = aim_ref[...]
    mag = jnp.exp(ar_all * dt_all)
    lre_all = mag * jnp.cos(ai_all * dt_all)
    lim_all = mag * jnp.sin(ai_all * dt_all)
    den = ar_all * ar_all + ai_all * ai_all
    fre_all = ((lre_all - 1.0) * ar_all + lim_all * ai_all) / den
    fim_all = (lim_all * ar_all - (lre_all - 1.0) * ai_all) / den
    for j in range(GROUP_BLOCK):
        lre, lim = lre_all[:, j:j + 1], lim_all[:, j:j + 1]
        fre, fim = fre_all[:, j:j + 1], fim_all[:, j:j + 1]
        bre = bre_t[j]
        bim = bim_t[j]
        bbre = fre * bre - fim * bim
        bbim = fre * bim + fim * bre
        pre, pim = _cpow(lre, lim, kexp, (chunk - 1).bit_length(), (SSM_STATE, width))
        wre = pre * bbre - pim * bbim
        wim = pre * bbim + pim * bbre
        w_ref[j, 0:SSM_STATE, :] = wre[:, :kl].astype(BF16)
        w_ref[j, SSM_STATE:2 * SSM_STATE, :] = wim[:, :kl].astype(BF16)
        ktab = (jnp.dot(cre[j], wre, precision=lax.Precision.HIGHEST, preferred_element_type=F32)
                - jnp.dot(cim[j], wim, precision=lax.Precision.HIGHEST, preferred_element_type=F32))
        for t in range(chunk):
            shift = (width - SSM_GROUP * (chunk - 1 - t)) % width
            r = pltpu.roll(ktab, shift, 1) if shift else ktab
            blk = jnp.where(lane < SSM_GROUP * (t + 1), r, 0.0)
            m_ref[j, SSM_GROUP * t:SSM_GROUP * (t + 1), :] = blk[:, :kl].astype(BF16)
        qre, qim = _cpow(lre, lim, kv, (width // SSM_GROUP).bit_length(), (SSM_STATE, width))
        ct_re = cre_tt[j]
        ct_im = cim_tt[j]
        v_t = jnp.concatenate([ct_re * qre - ct_im * qim, -(ct_re * qim + ct_im * qre)], axis=0)
        v_ref[j] = v_t.T[:kl, :].astype(BF16)
        sr, si = lre, lim
        for _ in range(chunk.bit_length() - 1):
            sr, si = sr * sr - si * si, 2.0 * sr * si
        dre = jnp.zeros((SSM_STATE, 8), F32)
        dim = jnp.zeros((SSM_STATE, 8), F32)
        for k in range(n_pow):
            dre = jnp.where(lane8 == k, sr, dre)
            dim = jnp.where(lane8 == k, si, dim)
            if k + 1 < n_pow:
                sr, si = sr * sr - si * si, 2.0 * sr * si
        dre_ref[j] = dre
        dim_ref[j] = dim


def _ssm_prep(a_re, a_im, log_dt, b_re, b_im, c_re, c_im, chunk, n_pow):
    g, p, h = SSM_GROUPS, SSM_STATE, SSM_GROUP
    kl = h * chunk
    width = max(kl, LANES)
    gb = GROUP_BLOCK
    reps = width // h
    cols = lambda a: jnp.transpose(a.reshape(g // gb, gb, p), (0, 2, 1))
    tile_t = lambda c: jnp.tile(jnp.swapaxes(c, 1, 2), (1, 1, reps))
    args = (cols(a_re), cols(a_im), log_dt.reshape(g // gb, 1, gb),
            jnp.tile(b_re, (1, 1, reps)), jnp.tile(b_im, (1, 1, reps)),
            c_re, c_im, tile_t(c_re), tile_t(c_im))

    def spec(shape):
        return pl.BlockSpec((gb,) + shape, lambda i: (i, 0, 0))

    def blockwise(shape):
        return pl.BlockSpec((None,) + shape, lambda i: (i, 0, 0))

    in_specs = [blockwise((p, gb)), blockwise((p, gb)), blockwise((1, gb)),
                spec((p, width)), spec((p, width)), spec((h, p)), spec((h, p)),
                spec((p, width)), spec((p, width))]
    out_shape = (jax.ShapeDtypeStruct((g, kl, kl), BF16), jax.ShapeDtypeStruct((g, 2 * p, kl), BF16),
                 jax.ShapeDtypeStruct((g, kl, 2 * p), BF16),
                 jax.ShapeDtypeStruct((g, p, 8), F32), jax.ShapeDtypeStruct((g, p, 8), F32))
    out_specs = (spec((kl, kl)), spec((2 * p, kl)), spec((kl, 2 * p)), spec((p, 8)), spec((p, 8)))
    return pl.pallas_call(
        functools.partial(_prep_kernel, chunk=chunk, width=width, n_pow=n_pow),
        grid=(g // gb,), in_specs=in_specs, out_specs=out_specs, out_shape=out_shape,
        compiler_params=_cparams(("arbitrary",)), name=f"s5_prep_{chunk}")(*args)


def _inproj0_kernel(x_ref, g_ref, w_ref, u_ref, gate_ref):
    xn = _rms_norm(x_ref[...], g_ref[...]).astype(BF16)
    res = lax.dot_general(w_ref[...], xn, (((1,), (1,)), ((), ())),
                          preferred_element_type=F32)
    u_ref[...] = res[:D_INNER].reshape(SSM_GROUPS, SSM_GROUP, -1).astype(BF16)
    gate_ref[...] = res[D_INNER:].astype(BF16)


def _inproj0(x2d, g, w_t, chunk, tc):
    nc = x2d.shape[0]
    return pl.pallas_call(
        _inproj0_kernel,
        grid=(nc // tc, chunk),
        in_specs=[pl.BlockSpec((tc, D_MODEL), lambda i, s: (i, s)),
                  pl.BlockSpec((1, D_MODEL), lambda i, s: (0, 0)),
                  pl.BlockSpec((2 * D_INNER, D_MODEL), lambda i, s: (0, 0))],
        out_specs=(pl.BlockSpec((SSM_GROUPS, SSM_GROUP, tc), lambda i, s: (0, s, i)),
                   pl.BlockSpec((None, D_INNER, tc), lambda i, s: (s, 0, i))),
        out_shape=(jax.ShapeDtypeStruct((SSM_GROUPS, SSM_GROUP * chunk, nc), BF16),
                   jax.ShapeDtypeStruct((chunk, D_INNER, nc), BF16)),
        compiler_params=_cparams(("arbitrary", "arbitrary")), name=f"l0_inproj_{chunk}")(x2d, g, w_t)


def _ssm_kernel(*refs, chunk, chunks_per_seq, carried):
    if carried:
        (u_ref, m_ref, w_ref, v_ref, dre_ref, dim_ref, d_ref, h0re_ref, h0im_ref,
         y_ref, hre_ref, him_ref, sre_ref, sim_ref) = refs
    else:
        (u_ref, m_ref, w_ref, v_ref, dre_ref, dim_ref, d_ref, y_ref, hre_ref, him_ref,
         sre_ref, sim_ref, cre_ref, cim_ref) = refs
    tc = u_ref.shape[-1]
    p = SSM_STATE
    lane = lax.broadcasted_iota(jnp.int32, (1, tc), 1)
    pos = lane & (chunks_per_seq - 1)
    groups = range(GROUP_BLOCK)
    n_steps = chunks_per_seq.bit_length() - 1
    if not carried:
        @pl.when(pl.program_id(1) == 0)
        def _():
            for k in range(n_steps):
                keep = pos >= (1 << k)
                for j in groups:
                    cre_ref[k, j] = jnp.where(keep, dre_ref[j][:, k:k + 1], 0.0)
                    cim_ref[k, j] = jnp.where(keep, dim_ref[j][:, k:k + 1], 0.0)
    for j in groups:
        z = jnp.dot(w_ref[j], u_ref[j], preferred_element_type=F32)
        sre_ref[j] = z[:p]
        sim_ref[j] = z[p:]
    if carried:
        for j in groups:
            pre_re, pre_im = h0re_ref[j], h0im_ref[j]
            ar, ai = dre_ref[j][:, 0:1], dim_ref[j][:, 0:1]
            hre_ref[j] = sre_ref[j] + ar * pre_re - ai * pre_im
            him_ref[j] = sim_ref[j] + ar * pre_im + ai * pre_re
    else:
        for k in range(n_steps):
            for j in groups:
                sre, sim = sre_ref[j], sim_ref[j]
                ar, ai = cre_ref[k, j], cim_ref[k, j]
                shr = pltpu.roll(sre, 1 << k, 1)
                shi = pltpu.roll(sim, 1 << k, 1)
                sre_ref[j] = sre + (ar * shr - ai * shi)
                sim_ref[j] = sim + (ar * shi + ai * shr)
        nseq = tc // chunks_per_seq
        lane_n = lax.broadcasted_iota(jnp.int32, (1, nseq), 1)
        for j in groups:
            sre, sim = sre_ref[j], sim_ref[j]
            fre = jnp.zeros((p, nseq), F32)
            fim = jnp.zeros((p, nseq), F32)
            for n in range(nseq):
                sel = lane == (n * chunks_per_seq + chunks_per_seq - 1)
                fre = jnp.where(lane_n == n, jnp.sum(jnp.where(sel, sre, 0.0), axis=1, keepdims=True), fre)
                fim = jnp.where(lane_n == n, jnp.sum(jnp.where(sel, sim, 0.0), axis=1, keepdims=True), fim)
            hre_ref[j, 0] = fre
            him_ref[j, 0] = fim
    first = pos == 0
    for j in groups:
        u = u_ref[j]
        if carried:
            pre_re, pre_im = h0re_ref[j], h0im_ref[j]
        else:
            pre_re = jnp.where(first, 0.0, pltpu.roll(sre_ref[j], 1, 1))
            pre_im = jnp.where(first, 0.0, pltpu.roll(sim_ref[j], 1, 1))
        prev = jnp.concatenate([pre_re, pre_im], axis=0).astype(BF16)
        y = (jnp.dot(m_ref[j], u, preferred_element_type=F32)
             + jnp.dot(v_ref[j], prev, preferred_element_type=F32)
             + d_ref[j] * u.astype(F32))
        y_ref[:, SSM_GROUP * j:SSM_GROUP * (j + 1), :] = (
            y.reshape(chunk, SSM_GROUP, tc).astype(BF16))


def _ssm(u2, m, w, v, dre, dim, d_col, h0, chunk, chunks_per_seq, tc):
    g, kl, nc = u2.shape
    gb, p = GROUP_BLOCK, SSM_STATE
    carried = h0 is not None
    nt = nc // tc
    grid = (g // gb, nt)
    wspec = lambda shape: pl.BlockSpec((gb,) + shape, lambda a, i: (a, 0, 0))
    in_specs = [pl.BlockSpec((gb, kl, tc), lambda a, i: (a, 0, i)),
                wspec((kl, kl)), wspec((2 * p, kl)), wspec((kl, 2 * p)),
                wspec((p, 8)), wspec((p, 8)), wspec((kl, 1))]
    args = [u2, m, w, v, dre, dim, d_col]
    if carried:
        in_specs += [pl.BlockSpec((gb, p, tc), lambda a, i: (a, 0, i))] * 2
        args += list(h0)
        hshape = jax.ShapeDtypeStruct((g, p, nc), F32)
        hspec = pl.BlockSpec((gb, p, tc), lambda a, i: (a, 0, i))
    else:
        nseq = tc // chunks_per_seq
        hshape = jax.ShapeDtypeStruct((g, nt, p, nseq), F32)
        hspec = pl.BlockSpec((gb, 1, p, nseq), lambda a, i: (a, i, 0, 0))
    scratch = [pltpu.VMEM((gb, p, tc), F32), pltpu.VMEM((gb, p, tc), F32)]
    if not carried:
        n_steps = chunks_per_seq.bit_length() - 1
        scratch += [pltpu.VMEM((n_steps, gb, p, tc), F32), pltpu.VMEM((n_steps, gb, p, tc), F32)]
    return pl.pallas_call(
        functools.partial(_ssm_kernel, chunk=chunk, chunks_per_seq=chunks_per_seq, carried=carried),
        grid=grid, in_specs=in_specs,
        out_specs=(pl.BlockSpec((chunk, SSM_GROUP * gb, tc), lambda a, i: (0, a, i)), hspec, hspec),
        out_shape=(jax.ShapeDtypeStruct((chunk, D_INNER, nc), BF16), hshape, hshape),
        scratch_shapes=scratch,
        compiler_params=_cparams(("arbitrary", "arbitrary")), name=f"s5_scan_{chunk}")(*args)


POST_PHASES = 2


def _post0_kernel(y_ref, gate_ref, x_ref, wglu_ref, bglu_ref, wout_ref, o_ref, *, phase_major_out):
    phases = range(POST_PHASES)
    ys = [y_ref[ph].astype(F32) for ph in phases]
    ys = [0.5 * y * (1.0 + lax.erf(y * math.sqrt(0.5))) for y in ys]
    zs = [jnp.dot(wglu_ref[...], y.astype(BF16), preferred_element_type=F32) + bglu_ref[...] for y in ys]
    ys = [y * _sigmoid(z) * _silu(gate_ref[ph].astype(F32)) for ph, y, z in zip(phases, ys, zs)]
    outs = [jnp.dot(wout_ref[...], y.astype(BF16), preferred_element_type=F32) for y in ys]
    for ph, o in zip(phases, outs):
        cols = slice(D_MODEL * ph, D_MODEL * (ph + 1))
        if phase_major_out:
            o_ref[ph] = x_ref[:, cols] + o.T
        else:
            o_ref[:, cols] = x_ref[:, cols] + o.T


def _post0(y2, gate2, x2d, wglu_t, bglu_col, wout_t, chunk, tc, phase_major_out):
    nc = x2d.shape[0]
    ph = POST_PHASES
    if phase_major_out:
        out_shape = jax.ShapeDtypeStruct((chunk, nc, D_MODEL), F32)
        out_spec = pl.BlockSpec((ph, tc, D_MODEL), lambda i, s: (s, i, 0))
    else:
        out_shape = jax.ShapeDtypeStruct((nc, chunk * D_MODEL), F32)
        out_spec = pl.BlockSpec((tc, ph * D_MODEL), lambda i, s: (i, s))
    return pl.pallas_call(
        functools.partial(_post0_kernel, phase_major_out=phase_major_out),
        grid=(nc // tc, chunk // ph),
        in_specs=[pl.BlockSpec((ph, D_INNER, tc), lambda i, s: (s, 0, i)),
                  pl.BlockSpec((ph, D_INNER, tc), lambda i, s: (s, 0, i)),
                  pl.BlockSpec((tc, ph * D_MODEL), lambda i, s: (i, s)),
                  pl.BlockSpec((D_INNER, D_INNER), lambda i, s: (0, 0)),
                  pl.BlockSpec((D_INNER, 1), lambda i, s: (0, 0)),
                  pl.BlockSpec((D_MODEL, D_INNER), lambda i, s: (0, 0))],
        out_specs=out_spec, out_shape=out_shape,
        compiler_params=_cparams(("arbitrary", "arbitrary")), name=f"l0_post_{chunk}")(
            y2, gate2, x2d, wglu_t, bglu_col, wout_t)


def _layer_norm_act(z, gate, lng, lnb):
    mu = jnp.mean(z, axis=-1, keepdims=True)
    zc = z - mu
    var = jnp.mean(zc * zc, axis=-1, keepdims=True)
    zn = zc * lax.rsqrt(var + EPS) * lng + lnb
    return _silu(zn) * _silu(gate)


CONV_COLS = 256
HIST_CHUNKS = 8


def _conv_taps(lp):
    taps = {}
    for t in range(lp):
        for k in range(CONV_WIDTH):
            o = t + k - CONV_HIST
            delta = (-o + lp - 1) // lp if o < 0 else 0
            taps[t, k] = (o + lp * delta, delta)
    return taps


def _conv_prompt_kernel(x_ref, g_ref, w_ref, dww_ref, dwb_ref, lng_ref, lnb_ref, wout_ref, gf_ref,
                        y_ref, cst_ref, xn_ref, vbuf_ref, vsh_ref, gate_ref, zc_ref, *, lp, ct, tiles):
    j = pl.program_id(1)
    hist = HIST_CHUNKS
    taps = _conv_taps(lp)
    shifted = sorted({sd for sd in taps.values() if sd[1] > 0})
    slot = {sd: i for i, sd in enumerate(shifted)}
    n_blocks = D_INNER // CONV_COLS

    @pl.when(j == 0)
    def _():
        vbuf_ref[:, 0:hist, :] = jnp.zeros((lp, hist, D_INNER), F32)

    g = g_ref[...]
    for s in range(lp):
        xn_ref[ct * s:ct * (s + 1), :] = _rms_norm(x_ref[:, D_MODEL * s:D_MODEL * (s + 1)], g).astype(BF16)

    def lanes(start):
        return pl.ds(pl.multiple_of(start, CONV_COLS), CONV_COLS)

    def project(c):
        xn = xn_ref[...]
        a = jnp.dot(xn, w_ref[:, lanes(c * CONV_COLS)], preferred_element_type=F32)
        b = jnp.dot(xn, w_ref[:, lanes(D_INNER + c * CONV_COLS)], preferred_element_type=F32)
        cols = lanes(c * CONV_COLS)
        gate_ref[:, cols] = jnp.dot(xn, w_ref[:, lanes(2 * D_INNER + c * CONV_COLS)],
                                    preferred_element_type=F32)
        v = a * _sigmoid(b)
        for s in range(lp):
            vbuf_ref[s, hist:hist + ct, cols] = v[ct * s:ct * (s + 1)]
        for (s, delta), i in slot.items():
            vsh_ref[i, :, cols] = vbuf_ref[s, hist - delta:hist - delta + ct, cols]

    def conv(c):
        cols = lanes(c * CONV_COLS)
        for t in range(lp):
            acc = jnp.broadcast_to(dwb_ref[:, cols], (ct, CONV_COLS))
            for k in range(CONV_WIDTH):
                s, delta = taps[t, k]
                slab = vsh_ref[slot[s, delta], :, cols] if delta else vbuf_ref[s, hist:hist + ct, cols]
                acc = acc + dww_ref[k:k + 1, cols] * slab
            zc_ref[ct * t:ct * (t + 1), cols] = acc

    project(0)

    def block(c, carry):
        conv(c)
        project(c + 1)
        return carry

    lax.fori_loop(0, n_blocks - 1, block, 0)
    conv(n_blocks - 1)
    zact = _layer_norm_act(zc_ref[...], gate_ref[...], lng_ref[...], lnb_ref[...]).astype(BF16)
    o = jnp.dot(zact, wout_ref[...], preferred_element_type=F32)
    gf = gf_ref[...]
    for t in range(lp):
        cols = slice(D_MODEL * t, D_MODEL * (t + 1))
        y_ref[:, cols] = _rms_norm(x_ref[:, cols] + o[ct * t:ct * (t + 1)], gf)

    @pl.when(j == tiles - 1)
    def _():
        for i in range(CONV_HIST):
            tok = lp * ct - CONV_HIST + i
            cst_ref[i:i + 1, :] = vbuf_ref[tok % lp, hist + tok // lp:hist + tok // lp + 1, :]

    vbuf_ref[:, 0:hist, :] = vbuf_ref[:, ct:ct + hist, :]


def _layer1_prompt(x2d, g, w_in, dw_w, dw_b, ln_g, ln_b, w_out, gf, n, lp, ct):
    nc = x2d.shape[0]
    tiles = nc // (n * ct)
    n_shift = len({sd for sd in _conv_taps(lp).values() if sd[1] > 0})
    const = lambda shape: pl.BlockSpec(shape, lambda a, j: (0,) * len(shape), pipeline_mode=pl.Buffered(1))
    return pl.pallas_call(
        functools.partial(_conv_prompt_kernel, lp=lp, ct=ct, tiles=tiles),
        grid=(n, tiles),
        in_specs=[pl.BlockSpec((ct, lp * D_MODEL), lambda a, j: (a * tiles + j, 0)),
                  const((1, D_MODEL)), const((D_MODEL, 3 * D_INNER)),
                  const((CONV_WIDTH, D_INNER)), const((1, D_INNER)),
                  const((1, D_INNER)), const((1, D_INNER)),
                  const((D_INNER, D_MODEL)), const((1, D_MODEL))],
        out_specs=(pl.BlockSpec((ct, lp * D_MODEL), lambda a, j: (a * tiles + j, 0)),
                   pl.BlockSpec((None, CONV_HIST, D_INNER), lambda a, j: (a, 0, 0))),
        out_shape=(jax.ShapeDtypeStruct((nc, lp * D_MODEL), F32),
                   jax.ShapeDtypeStruct((n, CONV_HIST, D_INNER), F32)),
        scratch_shapes=[pltpu.VMEM((lp * ct, D_MODEL), BF16),
                        pltpu.VMEM((lp, HIST_CHUNKS + ct, D_INNER), F32),
                        pltpu.VMEM((n_shift, ct, D_INNER), F32),
                        pltpu.VMEM((lp * ct, D_INNER), F32),
                        pltpu.VMEM((lp * ct, D_INNER), F32)],
        compiler_params=_cparams(("arbitrary", "arbitrary")), name="l1_prompt")(
            x2d, g, w_in, dw_w, dw_b, ln_g, ln_b, w_out, gf)


def _inproj1_kernel(x_ref, g_ref, w_ref, v_ref, gate_ref):
    xn = _rms_norm(x_ref[...], g_ref[...]).astype(BF16)
    abg = jnp.dot(xn, w_ref[...], preferred_element_type=F32)
    v_ref[...] = abg[:, :D_INNER] * _sigmoid(abg[:, D_INNER:2 * D_INNER])
    gate_ref[...] = abg[:, 2 * D_INNER:]


def _inproj1(x1, g, w_in, tm):
    r = x1.shape[0]
    const = lambda shape: pl.BlockSpec(shape, lambda i: (0,) * len(shape))
    return pl.pallas_call(
        _inproj1_kernel, grid=(r // tm,),
        in_specs=[pl.BlockSpec((tm, D_MODEL), lambda i: (i, 0)), const((1, D_MODEL)),
                  const((D_MODEL, 3 * D_INNER))],
        out_specs=(pl.BlockSpec((tm, D_INNER), lambda i: (i, 0)),) * 2,
        out_shape=(jax.ShapeDtypeStruct((r, D_INNER), F32),) * 2,
        compiler_params=_cparams(("arbitrary",)), name="l1_inproj_sample")(x1, g, w_in)


SAMPLE_SEQ_TILE = 16
SAMPLE_COLS = 1024


def _conv_sample_kernel(cache_ref, v_ref, gate_ref, dww_ref, dwb_ref, lng_ref, lnb_ref,
                        z_ref, cst_ref, acc_ref, *, steps):
    ns = cache_ref.shape[0]
    for t in range(steps):
        for c0 in range(0, D_INNER, SAMPLE_COLS):
            acc = jnp.broadcast_to(dwb_ref[:, c0:c0 + SAMPLE_COLS], (ns, SAMPLE_COLS))
            for k in range(CONV_WIDTH):
                jrow = t + k
                if jrow < CONV_HIST:
                    src = cache_ref[:, jrow * D_INNER + c0:jrow * D_INNER + c0 + SAMPLE_COLS]
                else:
                    src = v_ref[jrow - CONV_HIST, :, c0:c0 + SAMPLE_COLS]
                acc = acc + dww_ref[k:k + 1, c0:c0 + SAMPLE_COLS] * src
            acc_ref[t, :, c0:c0 + SAMPLE_COLS] = acc
        z_ref[t] = _layer_norm_act(acc_ref[t], gate_ref[t], lng_ref[...], lnb_ref[...]).astype(BF16)
    keep = CONV_HIST - steps
    cst_ref[:, 0:keep * D_INNER] = cache_ref[:, steps * D_INNER:CONV_HIST * D_INNER]
    for t in range(steps):
        cst_ref[:, (keep + t) * D_INNER:(keep + t + 1) * D_INNER] = v_ref[t]


def _conv_sample(cache2d, v3, gate3, dw_w, dw_b, ln_g, ln_b):
    steps, n, _ = v3.shape
    ns = SAMPLE_SEQ_TILE
    const = lambda shape: pl.BlockSpec(shape, lambda i: (0,) * len(shape))
    return pl.pallas_call(
        functools.partial(_conv_sample_kernel, steps=steps), grid=(n // ns,),
        in_specs=[pl.BlockSpec((ns, CONV_HIST * D_INNER), lambda i: (i, 0)),
                  pl.BlockSpec((steps, ns, D_INNER), lambda i: (0, i, 0)),
                  pl.BlockSpec((steps, ns, D_INNER), lambda i: (0, i, 0)),
                  const((CONV_WIDTH, D_INNER)), const((1, D_INNER)), const((1, D_INNER)),
                  const((1, D_INNER))],
        out_specs=(pl.BlockSpec((steps, ns, D_INNER), lambda i: (0, i, 0)),
                   pl.BlockSpec((ns, CONV_HIST * D_INNER), lambda i: (i, 0))),
        out_shape=(jax.ShapeDtypeStruct((steps, n, D_INNER), BF16),
                   jax.ShapeDtypeStruct((n, CONV_HIST * D_INNER), F32)),
        scratch_shapes=[pltpu.VMEM((steps, ns, D_INNER), F32)],
        compiler_params=_cparams(("arbitrary",)), name="l1_conv_sample")(
            cache2d, v3, gate3, dw_w, dw_b, ln_g, ln_b)


def _out1_kernel(z_ref, x_ref, w_ref, g_ref, y_ref):
    x2 = x_ref[...] + jnp.dot(z_ref[...], w_ref[...], preferred_element_type=F32)
    y_ref[...] = _rms_norm(x2, g_ref[...])


def _out1(z, x1, w_out, g, tm):
    r = x1.shape[0]
    phases = x1.shape[1] // D_MODEL
    const = lambda shape: pl.BlockSpec(shape, lambda i, s: (0,) * len(shape))
    return pl.pallas_call(
        _out1_kernel, grid=(r // tm, phases),
        in_specs=[pl.BlockSpec((tm, D_INNER), lambda i, s: (i, s)),
                  pl.BlockSpec((tm, D_MODEL), lambda i, s: (i, s)),
                  const((D_INNER, D_MODEL)), const((1, D_MODEL))],
        out_specs=pl.BlockSpec((tm, D_MODEL), lambda i, s: (i, s)),
        out_shape=jax.ShapeDtypeStruct((r, phases * D_MODEL), F32),
        compiler_params=_cparams(("arbitrary", "arbitrary")), name="l1_out")(z, x1, w_out, g)


def kernel(x_prompt, x_sample, state_ssm_re, state_ssm_im, cache_conv, norm_g, final_norm_g, ssm_w_in, ssm_a_re, ssm_a_im, ssm_log_dt, ssm_b_re, ssm_b_im, ssm_c_re, ssm_c_im, ssm_d, ssm_w_glu, ssm_b_glu, ssm_w_out, conv_w_in, conv_dw_w, conv_dw_b, conv_ln_g, conv_ln_b, conv_w_out):
    n_p, t_p, _ = x_prompt.shape
    n_s, t_s, _ = x_sample.shape
    g, p = SSM_GROUPS, SSM_STATE

    w_in0_t = ssm_w_in[0].T.astype(BF16)
    w_glu_t = ssm_w_glu[0].T.astype(BF16)
    b_glu_col = ssm_b_glu[0].reshape(D_INNER, 1)
    w_out0_t = ssm_w_out[0].T.astype(BF16)
    w_in1 = conv_w_in[0].astype(BF16)
    w_out1 = conv_w_out[0].astype(BF16)
    g0 = norm_g[0].reshape(1, D_MODEL)
    g1 = norm_g[1].reshape(1, D_MODEL)
    gf = final_norm_g.reshape(1, D_MODEL)
    dw_w, dw_b = conv_dw_w[0], conv_dw_b[0].reshape(1, D_INNER)
    ln_g, ln_b = conv_ln_g[0].reshape(1, D_INNER), conv_ln_b[0].reshape(1, D_INNER)

    def layer0(x2d, chunk, chunks_per_seq, tc, h0, phase_major_out):
        m, w, v, dre, dim = _ssm_prep(ssm_a_re[0], ssm_a_im[0], ssm_log_dt[0], ssm_b_re[0], ssm_b_im[0],
                                      ssm_c_re[0], ssm_c_im[0], chunk, max(1, chunks_per_seq.bit_length() - 1))
        d_col = jnp.tile(ssm_d[0].reshape(g, 1, SSM_GROUP), (1, chunk, 1)).reshape(g, chunk * SSM_GROUP, 1)
        u2, gate2 = _inproj0(x2d, g0, w_in0_t, chunk, tc)
        y2, hre, him = _ssm(u2, m, w, v, dre, dim, d_col, h0, chunk, chunks_per_seq, tc)
        x1 = _post0(y2, gate2, x2d, w_glu_t, b_glu_col, w_out0_t, chunk, tc, phase_major_out)
        return x1, hre, him

    lp = PROMPT_CHUNK
    cps = t_p // lp
    tc_p = 2 * cps
    xp2d = x_prompt.reshape(n_p * cps, lp * D_MODEL)
    x1p, hre_p, him_p = layer0(xp2d, lp, cps, tc_p, None, False)
    ssm_re_p = jnp.transpose(hre_p, (1, 3, 0, 2)).reshape(1, n_p, g, p)
    ssm_im_p = jnp.transpose(him_p, (1, 3, 0, 2)).reshape(1, n_p, g, p)
    y2d, conv_p = _layer1_prompt(x1p, g1, w_in1, dw_w, dw_b, ln_g, ln_b, w_out1, gf, n_p, lp, L1_CHUNK_ROWS)
    y_prompt = y2d.reshape(n_p, t_p, D_MODEL)

    h0 = (jnp.transpose(state_ssm_re[0], (1, 2, 0)), jnp.transpose(state_ssm_im[0], (1, 2, 0)))
    xs2d = x_sample.reshape(n_s, t_s * D_MODEL)
    x1s, hre_s, him_s = layer0(xs2d, t_s, 1, n_s, h0, True)
    ssm_re_s = jnp.transpose(hre_s, (2, 0, 1))[None]
    ssm_im_s = jnp.transpose(him_s, (2, 0, 1))[None]
    x1s = x1s.reshape(t_s * n_s, D_MODEL)
    v_s, gate_s = _inproj1(x1s, g1, w_in1, tm=256)
    zs, conv_s = _conv_sample(cache_conv[0].reshape(n_s, CONV_HIST * D_INNER),
                              v_s.reshape(t_s, n_s, D_INNER), gate_s.reshape(t_s, n_s, D_INNER),
                              dw_w, dw_b, ln_g, ln_b)
    y_s = _out1(zs.reshape(t_s * n_s, D_INNER), x1s, w_out1, gf, tm=t_s * n_s)
    y_sample = jnp.transpose(y_s.reshape(t_s, n_s, D_MODEL), (1, 0, 2))

    return (y_prompt, y_sample, ssm_re_p, ssm_im_p, conv_p[None],
            ssm_re_s, ssm_im_s, conv_s.reshape(1, n_s, CONV_HIST, D_INNER))
```

```python
import functools
import math

import jax
import jax.numpy as jnp
from jax import lax
from jax.experimental import pallas as pl
from jax.experimental.pallas import tpu as pltpu

D_MODEL = 1024
D_INNER = 2048
SSM_GROUP = 16
SSM_GROUPS = D_INNER // SSM_GROUP
SSM_STATE = 64
CONV_WIDTH = 31
CONV_HIST = CONV_WIDTH - 1
EPS = 1e-6

F32 = jnp.float32
BF16 = jnp.bfloat16

LANES = 128
PROMPT_CHUNK = 16
GROUP_BLOCK = 8
L1_CHUNK_ROWS = 32
VMEM_LIMIT = 56 * 1024 * 1024


def _cparams(sem):
    return pltpu.CompilerParams(dimension_semantics=sem, vmem_limit_bytes=VMEM_LIMIT)


def _rms_norm(x, g):
    ms = jnp.mean(x * x, axis=-1, keepdims=True)
    return x * lax.rsqrt(ms + EPS) * g


def _sigmoid(x):
    return 1.0 / (1.0 + jnp.exp(-x))


def _silu(x):
    return x * _sigmoid(x)


def _cpow(lre, lim, k, nbits, shape):
    pr = jnp.ones(shape, F32)
    pi = jnp.zeros(shape, F32)
    sr, si = lre, lim
    for b in range(nbits):
        take = (lax.shift_right_logical(k, b) & 1) == 1
        pr, pi = (jnp.where(take, pr * sr - pi * si, pr), jnp.where(take, pr * si + pi * sr, pi))
        if b + 1 < nbits:
            sr, si = sr * sr - si * si, 2.0 * sr * si
    return pr, pi


def _prep_kernel(are_ref, aim_ref, ldt_ref, bre_t, bim_t, cre, cim, cre_tt, cim_tt,
                 m_ref, w_ref, v_ref, dre_ref, dim_ref, *, chunk, width, n_pow, short):
    kl = SSM_GROUP * chunk
    lane = lax.broadcasted_iota(jnp.int32, (1, width), 1)
    step = lax.shift_right_logical(lane, 4)
    kexp = jnp.maximum(chunk - 1 - step, 0)
    kv = step + 1
    lane8 = lax.broadcasted_iota(jnp.int32, (1, 8), 1)
    dt_all = jnp.exp(ldt_ref[...])
    ar_all = are_ref[...]
    ai_all = aim_ref[...]
    mag = jnp.exp(ar_all * dt_all)
    lre_all = mag * jnp.cos(ai_all * dt_all)
    lim_all = mag * jnp.sin(ai_all * dt_all)
    den = ar_all * ar_all + ai_all * ai_all
    fre_all = ((lre_all - 1.0) * ar_all + lim_all * ai_all) / den
    fim_all = (lim_all * ar_all - (lre_all - 1.0) * ai_all) / den
    for j in range(GROUP_BLOCK):
        lre, lim = lre_all[:, j:j + 1], lim_all[:, j:j + 1]
        fre, fim = fre_all[:, j:j + 1], fim_all[:, j:j + 1]
        bre = bre_t[j]
        bim = bim_t[j]
        bbre = fre * bre - fim * bim
        bbim = fre * bim + fim * bre
        pre, pim = _cpow(lre, lim, kexp, (chunk - 1).bit_length(), (SSM_STATE, width))
        wre = pre * bbre - pim * bbim
        wim = pre * bbim + pim * bbre
        w_ref[j, 0:SSM_STATE, :] = wre[:, :kl].astype(BF16)
        w_ref[j, SSM_STATE:2 * SSM_STATE, :] = wim[:, :kl].astype(BF16)
        ktab = (jnp.dot(cre[j], wre, precision=lax.Precision.HIGHEST, preferred_element_type=F32)
                - jnp.dot(cim[j], wim, precision=lax.Precision.HIGHEST, preferred_element_type=F32))
        for t in range(chunk):
            shift = (width - SSM_GROUP * (chunk - 1 - t)) % width
            r = pltpu.roll(ktab, shift, 1) if shift else ktab
            blk = jnp.where(lane < SSM_GROUP * (t + 1), r, 0.0)
            m_ref[j, SSM_GROUP * t:SSM_GROUP * (t + 1), :] = blk[:, :kl].astype(BF16)
        qre, qim = _cpow(lre, lim, kv, (width // SSM_GROUP).bit_length(), (SSM_STATE, width))
        ct_re = cre_tt[j]
        ct_im = cim_tt[j]
        v_t = jnp.concatenate([ct_re * qre - ct_im * qim, -(ct_re * qim + ct_im * qre)], axis=0)
        v_ref[j] = v_t.T[:kl, :].astype(BF16)
        sr, si = lre, lim
        dre = jnp.zeros((SSM_STATE, 8), F32)
        dim = jnp.zeros((SSM_STATE, 8), F32)
        for e in range(chunk.bit_length() - 1):
            if (1 << e) == short:
                dre = jnp.where(lane8 == 7, sr, dre)
                dim = jnp.where(lane8 == 7, si, dim)
            sr, si = sr * sr - si * si, 2.0 * sr * si
        for k in range(n_pow):
            dre = jnp.where(lane8 == k, sr, dre)
            dim = jnp.where(lane8 == k, si, dim)
            if k + 1 < n_pow:
                sr, si = sr * sr - si * si, 2.0 * sr * si
        dre_ref[j] = dre
        dim_ref[j] = dim


def _ssm_prep(a_re, a_im, log_dt, b_re, b_im, c_re, c_im, chunk, n_pow, short):
    g, p, h = SSM_GROUPS, SSM_STATE, SSM_GROUP
    kl = h * chunk
    width = max(kl, LANES)
    gb = GROUP_BLOCK
    reps = width // h
    cols = lambda a: jnp.transpose(a.reshape(g // gb, gb, p), (0, 2, 1))
    tile_t = lambda c: jnp.tile(jnp.swapaxes(c, 1, 2), (1, 1, reps))
    args = (cols(a_re), cols(a_im), log_dt.reshape(g // gb, 1, gb),
            jnp.tile(b_re, (1, 1, reps)), jnp.tile(b_im, (1, 1, reps)),
            c_re, c_im, tile_t(c_re), tile_t(c_im))

    def spec(shape):
        return pl.BlockSpec((gb,) + shape, lambda i: (i, 0, 0))

    def blockwise(shape):
        return pl.BlockSpec((None,) + shape, lambda i: (i, 0, 0))

    in_specs = [blockwise((p, gb)), blockwise((p, gb)), blockwise((1, gb)),
                spec((p, width)), spec((p, width)), spec((h, p)), spec((h, p)),
                spec((p, width)), spec((p, width))]
    out_shape = (jax.ShapeDtypeStruct((g, kl, kl), BF16), jax.ShapeDtypeStruct((g, 2 * p, kl), BF16),
                 jax.ShapeDtypeStruct((g, kl, 2 * p), BF16),
                 jax.ShapeDtypeStruct((g, p, 8), F32), jax.ShapeDtypeStruct((g, p, 8), F32))
    out_specs = (spec((kl, kl)), spec((2 * p, kl)), spec((kl, 2 * p)), spec((p, 8)), spec((p, 8)))
    return pl.pallas_call(
        functools.partial(_prep_kernel, chunk=chunk, width=width, n_pow=n_pow, short=short),
        grid=(g // gb,), in_specs=in_specs, out_specs=out_specs, out_shape=out_shape,
        compiler_params=_cparams(("arbitrary",)), name=f"s5_prep_{chunk}")(*args)


def _phase_copies(x4_hbm, buf, sem, step, slot, *, chunk, per_step):
    cps = x4_hbm.shape[1]
    seqs = buf.shape[2] // cps
    copies = []
    for ph in range(per_step):
        gp = step * per_step + ph
        tile, s = gp // chunk, gp % chunk
        for q in range(seqs):
            copies.append(pltpu.make_async_copy(
                x4_hbm.at[tile * seqs + q, :, s, :],
                buf.at[slot, ph, pl.ds(q * cps, cps), :], sem.at[slot]))
    return copies


def _fetch_phases(x4_hbm, buf, sem, *, chunk, per_step):
    step = pl.program_id(0) * pl.num_programs(1) + pl.program_id(1)
    n_steps = pl.num_programs(0) * pl.num_programs(1)
    slot = step % 2
    copies = functools.partial(_phase_copies, x4_hbm, buf, sem, chunk=chunk, per_step=per_step)

    @pl.when(step == 0)
    def _():
        for c in copies(0, 0):
            c.start()

    @pl.when(step + 1 < n_steps)
    def _():
        for c in copies(step + 1, 1 - slot):
            c.start()

    for c in copies(step, slot):
        c.wait()
    return slot


def _inproj0_kernel(x4_hbm, g_ref, w_ref, u_ref, gate_ref, xbuf, sem, *, chunk):
    slot = _fetch_phases(x4_hbm, xbuf, sem, chunk=chunk, per_step=1)
    xn = _rms_norm(xbuf[slot, 0], g_ref[...]).astype(BF16)
    res = lax.dot_general(w_ref[...], xn, (((1,), (1,)), ((), ())),
                          preferred_element_type=F32)
    u_ref[...] = res[:D_INNER].reshape(SSM_GROUPS, SSM_GROUP, -1).astype(BF16)
    gate_ref[...] = res[D_INNER:].astype(BF16)


def _inproj0(x4, g, w_t, tc):
    nseq, cps, chunk, _ = x4.shape
    nc = nseq * cps
    return pl.pallas_call(
        functools.partial(_inproj0_kernel, chunk=chunk),
        grid=(nc // tc, chunk),
        in_specs=[pl.BlockSpec(memory_space=pl.ANY),
                  pl.BlockSpec((1, D_MODEL), lambda i, s: (0, 0)),
                  pl.BlockSpec((2 * D_INNER, D_MODEL), lambda i, s: (0, 0))],
        out_specs=(pl.BlockSpec((SSM_GROUPS, SSM_GROUP, tc), lambda i, s: (0, s, i)),
                   pl.BlockSpec((None, D_INNER, tc), lambda i, s: (s, 0, i))),
        out_shape=(jax.ShapeDtypeStruct((SSM_GROUPS, SSM_GROUP * chunk, nc), BF16),
                   jax.ShapeDtypeStruct((chunk, D_INNER, nc), BF16)),
        scratch_shapes=[pltpu.VMEM((2, 1, tc, D_MODEL), F32), pltpu.SemaphoreType.DMA((2,))],
        compiler_params=_cparams(("arbitrary", "arbitrary")), name=f"l0_inproj_{chunk}")(x4, g, w_t)


def _ssm_kernel(*refs, chunk, chunks_per_seq, carried):
    if carried:
        (u_ref, m_ref, w_ref, v_ref, dre_ref, dim_ref, d_ref, h0re_ref, h0im_ref,
         y_ref, hre_ref, him_ref, sre_ref, sim_ref) = refs
    else:
        (u_ref, m_ref, w_ref, v_ref, dre_ref, dim_ref, d_ref, y_ref, hre_ref, him_ref,
         sre_ref, sim_ref, cre_ref, cim_ref) = refs
    tc = u_ref.shape[-1]
    p = SSM_STATE
    lane = lax.broadcasted_iota(jnp.int32, (1, tc), 1)
    pos = lane & (chunks_per_seq - 1)
    groups = range(GROUP_BLOCK)
    n_steps = chunks_per_seq.bit_length() - 1
    if not carried:
        @pl.when(pl.program_id(1) == 0)
        def _():
            for k in range(n_steps):
                keep = pos >= (1 << k)
                for j in groups:
                    cre_ref[k, j] = jnp.where(keep, dre_ref[j][:, k:k + 1], 0.0)
                    cim_ref[k, j] = jnp.where(keep, dim_ref[j][:, k:k + 1], 0.0)
    for j in groups:
        z = jnp.dot(w_ref[j], u_ref[j], preferred_element_type=F32)
        sre_ref[j] = z[:p]
        sim_ref[j] = z[p:]
    if carried:
        for j in groups:
            pre_re, pre_im = h0re_ref[j], h0im_ref[j]
            ar, ai = dre_ref[j][:, 0:1], dim_ref[j][:, 0:1]
            hre_ref[j] = sre_ref[j] + ar * pre_re - ai * pre_im
            him_ref[j] = sim_ref[j] + ar * pre_im + ai * pre_re
    else:
        for k in range(n_steps):
            for j in groups:
                sre, sim = sre_ref[j], sim_ref[j]
                ar, ai = cre_ref[k, j], cim_ref[k, j]
                shr = pltpu.roll(sre, 1 << k, 1)
                shi = pltpu.roll(sim, 1 << k, 1)
                sre_ref[j] = sre + (ar * shr - ai * shi)
                sim_ref[j] = sim + (ar * shi + ai * shr)
        nseq = tc // chunks_per_seq
        lane_n = lax.broadcasted_iota(jnp.int32, (1, nseq), 1)
        for j in groups:
            sre, sim = sre_ref[j], sim_ref[j]
            fre = jnp.zeros((p, nseq), F32)
            fim = jnp.zeros((p, nseq), F32)
            for n in range(nseq):
                sel = lane == (n * chunks_per_seq + chunks_per_seq - 1)
                fre = jnp.where(lane_n == n, jnp.sum(jnp.where(sel, sre, 0.0), axis=1, keepdims=True), fre)
                fim = jnp.where(lane_n == n, jnp.sum(jnp.where(sel, sim, 0.0), axis=1, keepdims=True), fim)
            hre_ref[j, 0] = fre
            him_ref[j, 0] = fim
    first = pos == 0
    for j in groups:
        u = u_ref[j]
        if carried:
            pre_re, pre_im = h0re_ref[j], h0im_ref[j]
        else:
            pre_re = jnp.where(first, 0.0, pltpu.roll(sre_ref[j], 1, 1))
            pre_im = jnp.where(first, 0.0, pltpu.roll(sim_ref[j], 1, 1))
        prev = jnp.concatenate([pre_re, pre_im], axis=0).astype(BF16)
        y = (jnp.dot(m_ref[j], u, preferred_element_type=F32)
             + jnp.dot(v_ref[j], prev, preferred_element_type=F32)
             + d_ref[j] * u.astype(F32))
        y_ref[:, SSM_GROUP * j:SSM_GROUP * (j + 1), :] = (
            y.reshape(chunk, SSM_GROUP, tc).astype(BF16))


def _ssm(u2, m, w, v, dre, dim, d_col, h0, chunk, chunks_per_seq, tc):
    g, kl, nc = u2.shape
    gb, p = GROUP_BLOCK, SSM_STATE
    carried = h0 is not None
    nt = nc // tc
    grid = (g // gb, nt)
    wspec = lambda shape: pl.BlockSpec((gb,) + shape, lambda a, i: (a, 0, 0))
    in_specs = [pl.BlockSpec((gb, kl, tc), lambda a, i: (a, 0, i)),
                wspec((kl, kl)), wspec((2 * p, kl)), wspec((kl, 2 * p)),
                wspec((p, 8)), wspec((p, 8)), wspec((kl, 1))]
    args = [u2, m, w, v, dre, dim, d_col]
    if carried:
        in_specs += [pl.BlockSpec((gb, p, tc), lambda a, i: (a, 0, i))] * 2
        args += list(h0)
        hshape = jax.ShapeDtypeStruct((g, p, nc), F32)
        hspec = pl.BlockSpec((gb, p, tc), lambda a, i: (a, 0, i))
    else:
        nseq = tc // chunks_per_seq
        hshape = jax.ShapeDtypeStruct((g, nt, p, nseq), F32)
        hspec = pl.BlockSpec((gb, 1, p, nseq), lambda a, i: (a, i, 0, 0))
    scratch = [pltpu.VMEM((gb, p, tc), F32), pltpu.VMEM((gb, p, tc), F32)]
    if not carried:
        n_steps = chunks_per_seq.bit_length() - 1
        scratch += [pltpu.VMEM((n_steps, gb, p, tc), F32), pltpu.VMEM((n_steps, gb, p, tc), F32)]
    return pl.pallas_call(
        functools.partial(_ssm_kernel, chunk=chunk, chunks_per_seq=chunks_per_seq, carried=carried),
        grid=grid, in_specs=in_specs,
        out_specs=(pl.BlockSpec((chunk, SSM_GROUP * gb, tc), lambda a, i: (0, a, i)), hspec, hspec),
        out_shape=(jax.ShapeDtypeStruct((chunk, D_INNER, nc), BF16), hshape, hshape),
        scratch_shapes=scratch,
        compiler_params=_cparams(("arbitrary", "arbitrary")), name=f"s5_scan_{chunk}")(*args)


POST_PHASES = 2


def _post0_kernel(y_ref, gate_ref, x4_hbm, wglu_ref, bglu_ref, wout_ref, o_ref, xbuf, sem, *,
                  chunk, phase_major_out):
    slot = _fetch_phases(x4_hbm, xbuf, sem, chunk=chunk, per_step=POST_PHASES)
    phases = range(POST_PHASES)
    ys = [y_ref[ph].astype(F32) for ph in phases]
    ys = [0.5 * y * (1.0 + lax.erf(y * math.sqrt(0.5))) for y in ys]
    zs = [jnp.dot(wglu_ref[...], y.astype(BF16), preferred_element_type=F32) + bglu_ref[...] for y in ys]
    ys = [y * _sigmoid(z) * _silu(gate_ref[ph].astype(F32)) for ph, y, z in zip(phases, ys, zs)]
    outs = [jnp.dot(wout_ref[...], y.astype(BF16), preferred_element_type=F32) for y in ys]
    for ph, o in zip(phases, outs):
        if phase_major_out:
            o_ref[ph] = xbuf[slot, ph] + o.T
        else:
            o_ref[:, D_MODEL * ph:D_MODEL * (ph + 1)] = xbuf[slot, ph] + o.T


def _post0(y2, gate2, x4, wglu_t, bglu_col, wout_t, tc, phase_major_out):
    nseq, cps, chunk, _ = x4.shape
    nc = nseq * cps
    ph = POST_PHASES
    if phase_major_out:
        out_shape = jax.ShapeDtypeStruct((chunk, nc, D_MODEL), F32)
        out_spec = pl.BlockSpec((ph, tc, D_MODEL), lambda i, s: (s, i, 0))
    else:
        out_shape = jax.ShapeDtypeStruct((nc, chunk * D_MODEL), F32)
        out_spec = pl.BlockSpec((tc, ph * D_MODEL), lambda i, s: (i, s))
    return pl.pallas_call(
        functools.partial(_post0_kernel, chunk=chunk, phase_major_out=phase_major_out),
        grid=(nc // tc, chunk // ph),
        in_specs=[pl.BlockSpec((ph, D_INNER, tc), lambda i, s: (s, 0, i)),
                  pl.BlockSpec((ph, D_INNER, tc), lambda i, s: (s, 0, i)),
                  pl.BlockSpec(memory_space=pl.ANY),
                  pl.BlockSpec((D_INNER, D_INNER), lambda i, s: (0, 0)),
                  pl.BlockSpec((D_INNER, 1), lambda i, s: (0, 0)),
                  pl.BlockSpec((D_MODEL, D_INNER), lambda i, s: (0, 0))],
        out_specs=out_spec, out_shape=out_shape,
        scratch_shapes=[pltpu.VMEM((2, ph, tc, D_MODEL), F32), pltpu.SemaphoreType.DMA((2,))],
        compiler_params=_cparams(("arbitrary", "arbitrary")), name=f"l0_post_{chunk}")(
            y2, gate2, x4, wglu_t, bglu_col, wout_t)


def _layer_norm_act(z, gate, lng, lnb):
    mu = jnp.mean(z, axis=-1, keepdims=True)
    zc = z - mu
    var = jnp.mean(zc * zc, axis=-1, keepdims=True)
    zn = zc * lax.rsqrt(var + EPS) * lng + lnb
    return _silu(zn) * _silu(gate)


CONV_COLS = 256
HIST_CHUNKS = 8


def _conv_taps(lp):
    taps = {}
    for t in range(lp):
        for k in range(CONV_WIDTH):
            o = t + k - CONV_HIST
            delta = (-o + lp - 1) // lp if o < 0 else 0
            taps[t, k] = (o + lp * delta, delta)
    return taps


def _conv_prompt_kernel(x_ref, g_ref, w_ref, dww_ref, dwb_ref, lng_ref, lnb_ref, wout_ref, gf_ref,
                        y4_hbm, cst_ref, xn_ref, vbuf_ref, vsh_ref, gate_ref, zc_ref, ybuf, ysem,
                        *, lp, ct, tiles, n_seq):
    j = pl.program_id(1)
    hist = HIST_CHUNKS
    taps = _conv_taps(lp)
    shifted = sorted({sd for sd in taps.values() if sd[1] > 0})
    slot = {sd: i for i, sd in enumerate(shifted)}
    n_blocks = D_INNER // CONV_COLS

    @pl.when(j == 0)
    def _():
        vbuf_ref[:, 0:hist, :] = jnp.zeros((lp, hist, D_INNER), F32)

    g = g_ref[...]
    for s in range(lp):
        xn_ref[ct * s:ct * (s + 1), :] = _rms_norm(x_ref[:, D_MODEL * s:D_MODEL * (s + 1)], g).astype(BF16)

    def lanes(start):
        return pl.ds(pl.multiple_of(start, CONV_COLS), CONV_COLS)

    def project(c):
        xn = xn_ref[...]
        a = jnp.dot(xn, w_ref[:, lanes(c * CONV_COLS)], preferred_element_type=F32)
        b = jnp.dot(xn, w_ref[:, lanes(D_INNER + c * CONV_COLS)], preferred_element_type=F32)
        cols = lanes(c * CONV_COLS)
        gate_ref[:, cols] = jnp.dot(xn, w_ref[:, lanes(2 * D_INNER + c * CONV_COLS)],
                                    preferred_element_type=F32)
        v = a * _sigmoid(b)
        for s in range(lp):
            vbuf_ref[s, hist:hist + ct, cols] = v[ct * s:ct * (s + 1)]
        for (s, delta), i in slot.items():
            vsh_ref[i, :, cols] = vbuf_ref[s, hist - delta:hist - delta + ct, cols]

    def conv(c):
        cols = lanes(c * CONV_COLS)
        for t in range(lp):
            acc = jnp.broadcast_to(dwb_ref[:, cols], (ct, CONV_COLS))
            for k in range(CONV_WIDTH):
                s, delta = taps[t, k]
                slab = vsh_ref[slot[s, delta], :, cols] if delta else vbuf_ref[s, hist:hist + ct, cols]
                acc = acc + dww_ref[k:k + 1, cols] * slab
            zc_ref[ct * t:ct * (t + 1), cols] = acc

    project(0)

    def block(c, carry):
        conv(c)
        project(c + 1)
        return carry

    lax.fori_loop(0, n_blocks - 1, block, 0)
    conv(n_blocks - 1)
    zact = _layer_norm_act(zc_ref[...], gate_ref[...], lng_ref[...], lnb_ref[...]).astype(BF16)
    o = jnp.dot(zact, wout_ref[...], preferred_element_type=F32)
    gf = gf_ref[...]

    step = pl.program_id(0) * tiles + j
    n_steps = n_seq * tiles
    slot = step % 2

    def y_copies(stp, slt):
        seq, tile = stp // tiles, stp % tiles
        return [pltpu.make_async_copy(ybuf.at[slt, t], y4_hbm.at[seq, pl.ds(tile * ct, ct), t, :],
                                      ysem.at[slt]) for t in range(lp)]

    @pl.when(step >= 2)
    def _():
        for c in y_copies(step - 2, slot):
            c.wait()

    for t in range(lp):
        cols = slice(D_MODEL * t, D_MODEL * (t + 1))
        ybuf[slot, t] = _rms_norm(x_ref[:, cols] + o[ct * t:ct * (t + 1)], gf)
    for c in y_copies(step, slot):
        c.start()

    @pl.when(step == n_steps - 1)
    def _():
        if n_steps >= 2:
            for c in y_copies(step - 1, 1 - slot):
                c.wait()
        for c in y_copies(step, slot):
            c.wait()

    @pl.when(j == tiles - 1)
    def _():
        for i in range(CONV_HIST):
            tok = lp * ct - CONV_HIST + i
            cst_ref[i:i + 1, :] = vbuf_ref[tok % lp, hist + tok // lp:hist + tok // lp + 1, :]

    vbuf_ref[:, 0:hist, :] = vbuf_ref[:, ct:ct + hist, :]


def _layer1_prompt(x2d, g, w_in, dw_w, dw_b, ln_g, ln_b, w_out, gf, n, lp, ct):
    nc = x2d.shape[0]
    tiles = nc // (n * ct)
    n_shift = len({sd for sd in _conv_taps(lp).values() if sd[1] > 0})
    const = lambda shape: pl.BlockSpec(shape, lambda a, j: (0,) * len(shape), pipeline_mode=pl.Buffered(1))
    return pl.pallas_call(
        functools.partial(_conv_prompt_kernel, lp=lp, ct=ct, tiles=tiles, n_seq=n),
        grid=(n, tiles),
        in_specs=[pl.BlockSpec((ct, lp * D_MODEL), lambda a, j: (a * tiles + j, 0)),
                  const((1, D_MODEL)), const((D_MODEL, 3 * D_INNER)),
                  const((CONV_WIDTH, D_INNER)), const((1, D_INNER)),
                  const((1, D_INNER)), const((1, D_INNER)),
                  const((D_INNER, D_MODEL)), const((1, D_MODEL))],
        out_specs=(pl.BlockSpec(memory_space=pl.ANY),
                   pl.BlockSpec((None, CONV_HIST, D_INNER), lambda a, j: (a, 0, 0))),
        out_shape=(jax.ShapeDtypeStruct((n, nc // n, lp, D_MODEL), F32),
                   jax.ShapeDtypeStruct((n, CONV_HIST, D_INNER), F32)),
        scratch_shapes=[pltpu.VMEM((lp * ct, D_MODEL), BF16),
                        pltpu.VMEM((lp, HIST_CHUNKS + ct, D_INNER), F32),
                        pltpu.VMEM((n_shift, ct, D_INNER), F32),
                        pltpu.VMEM((lp * ct, D_INNER), F32),
                        pltpu.VMEM((lp * ct, D_INNER), F32),
                        pltpu.VMEM((2, lp, ct, D_MODEL), F32),
                        pltpu.SemaphoreType.DMA((2,))],
        compiler_params=_cparams(("arbitrary", "arbitrary")), name="l1_prompt")(
            x2d, g, w_in, dw_w, dw_b, ln_g, ln_b, w_out, gf)


def _inproj1_kernel(x_ref, g_ref, w_ref, v_ref, gate_ref):
    xn = _rms_norm(x_ref[...], g_ref[...]).astype(BF16)
    abg = jnp.dot(xn, w_ref[...], preferred_element_type=F32)
    v_ref[...] = abg[:, :D_INNER] * _sigmoid(abg[:, D_INNER:2 * D_INNER])
    gate_ref[...] = abg[:, 2 * D_INNER:]


def _inproj1(x1, g, w_in, tm):
    r = x1.shape[0]
    const = lambda shape: pl.BlockSpec(shape, lambda i: (0,) * len(shape))
    return pl.pallas_call(
        _inproj1_kernel, grid=(r // tm,),
        in_specs=[pl.BlockSpec((tm, D_MODEL), lambda i: (i, 0)), const((1, D_MODEL)),
                  const((D_MODEL, 3 * D_INNER))],
        out_specs=(pl.BlockSpec((tm, D_INNER), lambda i: (i, 0)),) * 2,
        out_shape=(jax.ShapeDtypeStruct((r, D_INNER), F32),) * 2,
        compiler_params=_cparams(("arbitrary",)), name="l1_inproj_sample")(x1, g, w_in)


SAMPLE_SEQ_TILE = 16
SAMPLE_COLS = 1024


def _conv_sample_kernel(cache_ref, v_ref, gate_ref, dww_ref, dwb_ref, lng_ref, lnb_ref,
                        z_ref, cst_ref, acc_ref, *, steps):
    ns = cache_ref.shape[0]
    for t in range(steps):
        for c0 in range(0, D_INNER, SAMPLE_COLS):
            acc = jnp.broadcast_to(dwb_ref[:, c0:c0 + SAMPLE_COLS], (ns, SAMPLE_COLS))
            for k in range(CONV_WIDTH):
                jrow = t + k
                if jrow < CONV_HIST:
                    src = cache_ref[:, jrow * D_INNER + c0:jrow * D_INNER + c0 + SAMPLE_COLS]
                else:
                    src = v_ref[jrow - CONV_HIST, :, c0:c0 + SAMPLE_COLS]
                acc = acc + dww_ref[k:k + 1, c0:c0 + SAMPLE_COLS] * src
            acc_ref[t, :, c0:c0 + SAMPLE_COLS] = acc
        z_ref[t] = _layer_norm_act(acc_ref[t], gate_ref[t], lng_ref[...], lnb_ref[...]).astype(BF16)
    keep = CONV_HIST - steps
    cst_ref[:, 0:keep * D_INNER] = cache_ref[:, steps * D_INNER:CONV_HIST * D_INNER]
    for t in range(steps):
        cst_ref[:, (keep + t) * D_INNER:(keep + t + 1) * D_INNER] = v_ref[t]


def _conv_sample(cache2d, v3, gate3, dw_w, dw_b, ln_g, ln_b):
    steps, n, _ = v3.shape
    ns = SAMPLE_SEQ_TILE
    const = lambda shape: pl.BlockSpec(shape, lambda i: (0,) * len(shape))
    return pl.pallas_call(
        functools.partial(_conv_sample_kernel, steps=steps), grid=(n // ns,),
        in_specs=[pl.BlockSpec((ns, CONV_HIST * D_INNER), lambda i: (i, 0)),
                  pl.BlockSpec((steps, ns, D_INNER), lambda i: (0, i, 0)),
                  pl.BlockSpec((steps, ns, D_INNER), lambda i: (0, i, 0)),
                  const((CONV_WIDTH, D_INNER)), const((1, D_INNER)), const((1, D_INNER)),
                  const((1, D_INNER))],
        out_specs=(pl.BlockSpec((steps, ns, D_INNER), lambda i: (0, i, 0)),
                   pl.BlockSpec((ns, CONV_HIST * D_INNER), lambda i: (i, 0))),
        out_shape=(jax.ShapeDtypeStruct((steps, n, D_INNER), BF16),
                   jax.ShapeDtypeStruct((n, CONV_HIST * D_INNER), F32)),
        scratch_shapes=[pltpu.VMEM((steps, ns, D_INNER), F32)],
        compiler_params=_cparams(("arbitrary",)), name="l1_conv_sample")(
            cache2d, v3, gate3, dw_w, dw_b, ln_g, ln_b)


def _out1_kernel(z_ref, x_ref, w_ref, g_ref, y_ref):
    x2 = x_ref[...] + jnp.dot(z_ref[...], w_ref[...], preferred_element_type=F32)
    y_ref[...] = _rms_norm(x2, g_ref[...])


def _out1(z, x1, w_out, g, tm):
    r = x1.shape[0]
    phases = x1.shape[1] // D_MODEL
    const = lambda shape: pl.BlockSpec(shape, lambda i, s: (0,) * len(shape))
    return pl.pallas_call(
        _out1_kernel, grid=(r // tm, phases),
        in_specs=[pl.BlockSpec((tm, D_INNER), lambda i, s: (i, s)),
                  pl.BlockSpec((tm, D_MODEL), lambda i, s: (i, s)),
                  const((D_INNER, D_MODEL)), const((1, D_MODEL))],
        out_specs=pl.BlockSpec((tm, D_MODEL), lambda i, s: (i, s)),
        out_shape=jax.ShapeDtypeStruct((r, phases * D_MODEL), F32),
        compiler_params=_cparams(("arbitrary", "arbitrary")), name="l1_out")(z, x1, w_out, g)


def kernel(x_prompt, x_sample, state_ssm_re, state_ssm_im, cache_conv, norm_g, final_norm_g, ssm_w_in, ssm_a_re, ssm_a_im, ssm_log_dt, ssm_b_re, ssm_b_im, ssm_c_re, ssm_c_im, ssm_d, ssm_w_glu, ssm_b_glu, ssm_w_out, conv_w_in, conv_dw_w, conv_dw_b, conv_ln_g, conv_ln_b, conv_w_out):
    n_p, t_p, _ = x_prompt.shape
    n_s, t_s, _ = x_sample.shape
    g, p = SSM_GROUPS, SSM_STATE

    w_in0_t = ssm_w_in[0].T.astype(BF16)
    w_glu_t = ssm_w_glu[0].T.astype(BF16)
    b_glu_col = ssm_b_glu[0].reshape(D_INNER, 1)
    w_out0_t = ssm_w_out[0].T.astype(BF16)
    w_in1 = conv_w_in[0].astype(BF16)
    w_out1 = conv_w_out[0].astype(BF16)
    g0 = norm_g[0].reshape(1, D_MODEL)
    g1 = norm_g[1].reshape(1, D_MODEL)
    gf = final_norm_g.reshape(1, D_MODEL)
    dw_w, dw_b = conv_dw_w[0], conv_dw_b[0].reshape(1, D_INNER)
    ln_g, ln_b = conv_ln_g[0].reshape(1, D_INNER), conv_ln_b[0].reshape(1, D_INNER)

    def layer0(x4, ops, chunks_per_seq, tc, h0, phase_major_out):
        chunk = x4.shape[2]
        m, w, v, dre, dim = ops
        d_col = jnp.tile(ssm_d[0].reshape(g, 1, SSM_GROUP), (1, chunk, 1)).reshape(g, chunk * SSM_GROUP, 1)
        u2, gate2 = _inproj0(x4, g0, w_in0_t, tc)
        y2, hre, him = _ssm(u2, m, w, v, dre, dim, d_col, h0, chunk, chunks_per_seq, tc)
        x1 = _post0(y2, gate2, x4, w_glu_t, b_glu_col, w_out0_t, tc, phase_major_out)
        return x1, hre, him

    lp = PROMPT_CHUNK
    cps = t_p // lp
    tc_p = 2 * cps
    assert t_s <= lp and lp % t_s == 0
    ops_p = _ssm_prep(ssm_a_re[0], ssm_a_im[0], ssm_log_dt[0], ssm_b_re[0], ssm_b_im[0],
                      ssm_c_re[0], ssm_c_im[0], lp, cps.bit_length() - 1, t_s)
    x1p, hre_p, him_p = layer0(x_prompt.reshape(n_p, cps, lp, D_MODEL), ops_p, cps, tc_p, None, False)
    ssm_re_p = jnp.transpose(hre_p, (1, 3, 0, 2)).reshape(1, n_p, g, p)
    ssm_im_p = jnp.transpose(him_p, (1, 3, 0, 2)).reshape(1, n_p, g, p)
    y4, conv_p = _layer1_prompt(x1p, g1, w_in1, dw_w, dw_b, ln_g, ln_b, w_out1, gf, n_p, lp, L1_CHUNK_ROWS)
    y_prompt = y4.reshape(n_p, t_p, D_MODEL)

    h0 = (jnp.transpose(state_ssm_re[0], (1, 2, 0)), jnp.transpose(state_ssm_im[0], (1, 2, 0)))
    m_p, w_p, v_p, dre_p, dim_p = ops_p
    kl_s = SSM_GROUP * t_s
    short_decay = lambda d: jnp.broadcast_to(d[:, :, 7:8], d.shape)
    ops_s = (m_p[:, :kl_s, :kl_s], w_p[:, :, SSM_GROUP * lp - kl_s:], v_p[:, :kl_s, :],
             short_decay(dre_p), short_decay(dim_p))
    x1s, hre_s, him_s = layer0(x_sample.reshape(1, n_s, t_s, D_MODEL), ops_s, 1, n_s, h0, True)
    ssm_re_s = jnp.transpose(hre_s, (2, 0, 1))[None]
    ssm_im_s = jnp.transpose(him_s, (2, 0, 1))[None]
    x1s = x1s.reshape(t_s * n_s, D_MODEL)
    v_s, gate_s = _inproj1(x1s, g1, w_in1, tm=256)
    zs, conv_s = _conv_sample(cache_conv[0].reshape(n_s, CONV_HIST * D_INNER),
                              v_s.reshape(t_s, n_s, D_INNER), gate_s.reshape(t_s, n_s, D_INNER),
                              dw_w, dw_b, ln_g, ln_b)
    y_s = _out1(zs.reshape(t_s * n_s, D_INNER), x1s, w_out1, gf, tm=t_s * n_s)
    y_sample = jnp.transpose(y_s.reshape(t_s, n_s, D_MODEL), (1, 0, 2))

    return (y_prompt, y_sample, ssm_re_p, ssm_im_p, conv_p[None],
            ssm_re_s, ssm_im_s, conv_s.reshape(1, n_s, CONV_HIST, D_INNER))
```

```python
import functools
import math

import jax
import jax.numpy as jnp
from jax import lax
from jax.experimental import pallas as pl
from jax.experimental.pallas import tpu as pltpu

D_MODEL = 1024
D_INNER = 2048
SSM_GROUP = 16
SSM_GROUPS = D_INNER // SSM_GROUP
SSM_STATE = 64
CONV_WIDTH = 31
CONV_HIST = CONV_WIDTH - 1
EPS = 1e-6

F32 = jnp.float32
BF16 = jnp.bfloat16

LANES = 128
PROMPT_CHUNK = 16
GROUP_BLOCK = 8
SCAN_TILES = 4
L1_CHUNK_ROWS = 32
VMEM_LIMIT = 56 * 1024 * 1024


def _cparams(sem):
    return pltpu.CompilerParams(dimension_semantics=sem, vmem_limit_bytes=VMEM_LIMIT)


def _rms_norm(x, g):
    ms = jnp.mean(x * x, axis=-1, keepdims=True)
    return x * lax.rsqrt(ms + EPS) * g


def _sigmoid(x):
    return 1.0 / (1.0 + jnp.exp(-x))


def _silu(x):
    return x * _sigmoid(x)


def _cpow(lre, lim, k, nbits, shape):
    pr = jnp.ones(shape, F32)
    pi = jnp.zeros(shape, F32)
    sr, si = lre, lim
    for b in range(nbits):
        take = (lax.shift_right_logical(k, b) & 1) == 1
        pr, pi = (jnp.where(take, pr * sr - pi * si, pr), jnp.where(take, pr * si + pi * sr, pi))
        if b + 1 < nbits:
            sr, si = sr * sr - si * si, 2.0 * sr * si
    return pr, pi


def _prep_kernel(are_ref, aim_ref, ldt_ref, bre_t, bim_t, cre, cim, cre_tt, cim_tt,
                 m_ref, w_ref, v_ref, dre_ref, dim_ref, *, chunk, width, n_pow, short):
    kl = SSM_GROUP * chunk
    lane = lax.broadcasted_iota(jnp.int32, (1, width), 1)
    step = lax.shift_right_logical(lane, 4)
    kexp = jnp.maximum(chunk - 1 - step, 0)
    kv = step + 1
    lane8 = lax.broadcasted_iota(jnp.int32, (1, 8), 1)
    dt_all = jnp.exp(ldt_ref[...])
    ar_all = are_ref[...]
    ai_all = aim_ref[...]
    mag = jnp.exp(ar_all * dt_all)
    lre_all = mag * jnp.cos(ai_all * dt_all)
    lim_all = mag * jnp.sin(ai_all * dt_all)
    den = ar_all * ar_all + ai_all * ai_all
    fre_all = ((lre_all - 1.0) * ar_all + lim_all * ai_all) / den
    fim_all = (lim_all * ar_all - (lre_all - 1.0) * ai_all) / den
    for j in range(GROUP_BLOCK):
        lre, lim = lre_all[:, j:j + 1], lim_all[:, j:j + 1]
        fre, fim = fre_all[:, j:j + 1], fim_all[:, j:j + 1]
        bre = bre_t[j]
        bim = bim_t[j]
        bbre = fre * bre - fim * bim
        bbim = fre * bim + fim * bre
        pre, pim = _cpow(lre, lim, kexp, (chunk - 1).bit_length(), (SSM_STATE, width))
        wre = pre * bbre - pim * bbim
        wim = pre * bbim + pim * bbre
        w_ref[j, 0:SSM_STATE, :] = wre[:, :kl].astype(BF16)
        w_ref[j, SSM_STATE:2 * SSM_STATE, :] = wim[:, :kl].astype(BF16)
        ktab = (jnp.dot(cre[j], wre, precision=lax.Precision.HIGHEST, preferred_element_type=F32)
                - jnp.dot(cim[j], wim, precision=lax.Precision.HIGHEST, preferred_element_type=F32))
        for t in range(chunk):
            shift = (width - SSM_GROUP * (chunk - 1 - t)) % width
            r = pltpu.roll(ktab, shift, 1) if shift else ktab
            blk = jnp.where(lane < SSM_GROUP * (t + 1), r, 0.0)
            m_ref[j, SSM_GROUP * t:SSM_GROUP * (t + 1), :] = blk[:, :kl].astype(BF16)
        qre, qim = _cpow(lre, lim, kv, (width // SSM_GROUP).bit_length(), (SSM_STATE, width))
        ct_re = cre_tt[j]
        ct_im = cim_tt[j]
        v_t = jnp.concatenate([ct_re * qre - ct_im * qim, -(ct_re * qim + ct_im * qre)], axis=0)
        v_ref[j] = v_t.T[:kl, :].astype(BF16)
        sr, si = lre, lim
        dre = jnp.zeros((SSM_STATE, 8), F32)
        dim = jnp.zeros((SSM_STATE, 8), F32)
        for e in range(chunk.bit_length() - 1):
            if (1 << e) == short:
                dre = jnp.where(lane8 == 7, sr, dre)
                dim = jnp.where(lane8 == 7, si, dim)
            sr, si = sr * sr - si * si, 2.0 * sr * si
        for k in range(n_pow):
            dre = jnp.where(lane8 == k, sr, dre)
            dim = jnp.where(lane8 == k, si, dim)
            if k + 1 < n_pow:
                sr, si = sr * sr - si * si, 2.0 * sr * si
        dre_ref[j] = dre
        dim_ref[j] = dim


def _ssm_prep(a_re, a_im, log_dt, b_re, b_im, c_re, c_im, chunk, n_pow, short):
    g, p, h = SSM_GROUPS, SSM_STATE, SSM_GROUP
    kl = h * chunk
    width = max(kl, LANES)
    gb = GROUP_BLOCK
    reps = width // h
    cols = lambda a: jnp.transpose(a.reshape(g // gb, gb, p), (0, 2, 1))
    tile_t = lambda c: jnp.tile(jnp.swapaxes(c, 1, 2), (1, 1, reps))
    args = (cols(a_re), cols(a_im), log_dt.reshape(g // gb, 1, gb),
            jnp.tile(b_re, (1, 1, reps)), jnp.tile(b_im, (1, 1, reps)),
            c_re, c_im, tile_t(c_re), tile_t(c_im))

    def spec(shape):
        return pl.BlockSpec((gb,) + shape, lambda i: (i, 0, 0))

    def blockwise(shape):
        return pl.BlockSpec((None,) + shape, lambda i: (i, 0, 0))

    in_specs = [blockwise((p, gb)), blockwise((p, gb)), blockwise((1, gb)),
                spec((p, width)), spec((p, width)), spec((h, p)), spec((h, p)),
                spec((p, width)), spec((p, width))]
    out_shape = (jax.ShapeDtypeStruct((g, kl, kl), BF16), jax.ShapeDtypeStruct((g, 2 * p, kl), BF16),
                 jax.ShapeDtypeStruct((g, kl, 2 * p), BF16),
                 jax.ShapeDtypeStruct((g, p, 8), F32), jax.ShapeDtypeStruct((g, p, 8), F32))
    out_specs = (spec((kl, kl)), spec((2 * p, kl)), spec((kl, 2 * p)), spec((p, 8)), spec((p, 8)))
    return pl.pallas_call(
        functools.partial(_prep_kernel, chunk=chunk, width=width, n_pow=n_pow, short=short),
        grid=(g // gb,), in_specs=in_specs, out_specs=out_specs, out_shape=out_shape,
        compiler_params=_cparams(("arbitrary",)), name=f"s5_prep_{chunk}")(*args)


def _phase_copies(x5_hbm, buf, sem, step, slot, *, per_step, to_hbm=False):
    nseq, nq, _, chunk, _ = x5_hbm.shape
    copies = []
    for ph in range(per_step):
        gp = step * per_step + ph
        r, s = gp // chunk, gp % chunk
        for seq in range(nseq):
            hbm = x5_hbm.at[seq, :, r, s, :]
            vmem = buf.at[slot, ph, pl.ds(seq * nq, nq), :]
            copies.append(pltpu.make_async_copy(vmem, hbm, sem.at[slot]) if to_hbm
                          else pltpu.make_async_copy(hbm, vmem, sem.at[slot]))
    return copies


def _grid_step():
    return (pl.program_id(0) * pl.num_programs(1) + pl.program_id(1),
            pl.num_programs(0) * pl.num_programs(1))


def _fetch_phases(x5_hbm, buf, sem, *, per_step):
    step, n_steps = _grid_step()
    slot = step % 2
    copies = functools.partial(_phase_copies, x5_hbm, buf, sem, per_step=per_step)

    @pl.when(step == 0)
    def _():
        for c in copies(0, 0):
            c.start()

    @pl.when(step + 1 < n_steps)
    def _():
        for c in copies(step + 1, 1 - slot):
            c.start()

    for c in copies(step, slot):
        c.wait()
    return slot


def _store_phases(o5_hbm, buf, sem, fill, *, per_step):
    step, n_steps = _grid_step()
    slot = step % 2
    copies = functools.partial(_phase_copies, o5_hbm, buf, sem, per_step=per_step, to_hbm=True)

    @pl.when(step >= 2)
    def _():
        for c in copies(step - 2, slot):
            c.wait()

    fill(slot)
    for c in copies(step, slot):
        c.start()

    @pl.when(step == n_steps - 1)
    def _():
        @pl.when(step >= 1)
        def _():
            for c in copies(step - 1, 1 - slot):
                c.wait()
        for c in copies(step, slot):
            c.wait()


def _inproj0_kernel(x5_hbm, g_ref, w_ref, u_ref, gate_ref, xbuf, sem):
    slot = _fetch_phases(x5_hbm, xbuf, sem, per_step=1)
    xn = _rms_norm(xbuf[slot, 0], g_ref[...]).astype(BF16)
    res = lax.dot_general(w_ref[...], xn, (((1,), (1,)), ((), ())),
                          preferred_element_type=F32)
    u_ref[...] = res[:D_INNER].reshape(SSM_GROUPS, SSM_GROUP, -1).astype(BF16)
    gate_ref[...] = res[D_INNER:].astype(BF16)


def _inproj0(x5, g, w_t):
    nseq, nq, n_tiles, chunk, _ = x5.shape
    tc = nseq * nq
    nc = tc * n_tiles
    return pl.pallas_call(
        _inproj0_kernel,
        grid=(n_tiles, chunk),
        in_specs=[pl.BlockSpec(memory_space=pl.ANY),
                  pl.BlockSpec((1, D_MODEL), lambda i, s: (0, 0)),
                  pl.BlockSpec((2 * D_INNER, D_MODEL), lambda i, s: (0, 0))],
        out_specs=(pl.BlockSpec((SSM_GROUPS, SSM_GROUP, tc), lambda i, s: (0, s, i)),
                   pl.BlockSpec((None, D_INNER, tc), lambda i, s: (s, 0, i))),
        out_shape=(jax.ShapeDtypeStruct((SSM_GROUPS, SSM_GROUP * chunk, nc), BF16),
                   jax.ShapeDtypeStruct((chunk, D_INNER, nc), BF16)),
        scratch_shapes=[pltpu.VMEM((2, 1, tc, D_MODEL), F32), pltpu.SemaphoreType.DMA((2,))],
        compiler_params=_cparams(("arbitrary", "arbitrary")), name=f"l0_inproj_{chunk}")(x5, g, w_t)


def _ssm_kernel(*refs, chunk, n_tiles, segs, carried):
    if carried:
        (u_ref, m_ref, w_ref, v_ref, dre_ref, dim_ref, d_ref, h0re_ref, h0im_ref,
         y_ref, hre_ref, him_ref, sre_ref, sim_ref) = refs
    else:
        (u_ref, m_ref, w_ref, v_ref, dre_ref, dim_ref, d_ref, y_ref, hre_ref, him_ref,
         sre_ref, sim_ref, ere_ref, eim_ref) = refs
    nc = u_ref.shape[-1]
    tl = nc // n_tiles
    p = SSM_STATE
    groups = range(GROUP_BLOCK)

    def tile(r):
        return slice(r * tl, (r + 1) * tl)

    def cmul(ar, ai, br, bi):
        return ar * br - ai * bi, ar * bi + ai * br

    for j in groups:
        z = jnp.dot(w_ref[j], u_ref[j], preferred_element_type=F32)
        sre_ref[j] = z[:p]
        sim_ref[j] = z[p:]
    if carried:
        for j in groups:
            dr, di = cmul(dre_ref[j][:, 0:1], dim_ref[j][:, 0:1], h0re_ref[j], h0im_ref[j])
            hre_ref[j] = sre_ref[j] + dr
            him_ref[j] = sim_ref[j] + di
    else:
        lane = lax.broadcasted_iota(jnp.int32, (1, tl), 1)
        posq = lane & (segs - 1)
        for r in range(1, n_tiles):
            for j in groups:
                dr, di = cmul(dre_ref[j][:, 0:1], dim_ref[j][:, 0:1],
                              sre_ref[j, :, tile(r - 1)], sim_ref[j, :, tile(r - 1)])
                sre_ref[j, :, tile(r)] = sre_ref[j, :, tile(r)] + dr
                sim_ref[j, :, tile(r)] = sim_ref[j, :, tile(r)] + di
        for j in groups:
            ere_ref[j] = sre_ref[j, :, tile(n_tiles - 1)]
            eim_ref[j] = sim_ref[j, :, tile(n_tiles - 1)]
        col0 = n_tiles.bit_length() - 1
        for k in range(segs.bit_length() - 1):
            keep = posq >= (1 << k)
            for j in groups:
                ar = jnp.where(keep, dre_ref[j][:, col0 + k:col0 + k + 1], 0.0)
                ai = jnp.where(keep, dim_ref[j][:, col0 + k:col0 + k + 1], 0.0)
                t_re, t_im = ere_ref[j], eim_ref[j]
                dr, di = cmul(ar, ai, pltpu.roll(t_re, 1 << k, 1), pltpu.roll(t_im, 1 << k, 1))
                ere_ref[j] = t_re + dr
                eim_ref[j] = t_im + di
        first = posq == 0
        for j in groups:
            hre_ref[j] = ere_ref[j]
            him_ref[j] = eim_ref[j]
            ere_ref[j] = jnp.where(first, 0.0, pltpu.roll(ere_ref[j], 1, 1))
            eim_ref[j] = jnp.where(first, 0.0, pltpu.roll(eim_ref[j], 1, 1))
        for j in groups:
            a_re, a_im = dre_ref[j][:, 0:1], dim_ref[j][:, 0:1]
            pw_re, pw_im = a_re, a_im
            e_re, e_im = ere_ref[j], eim_ref[j]
            for r in range(n_tiles):
                dr, di = cmul(pw_re, pw_im, e_re, e_im)
                sre_ref[j, :, tile(r)] = sre_ref[j, :, tile(r)] + dr
                sim_ref[j, :, tile(r)] = sim_ref[j, :, tile(r)] + di
                pw_re, pw_im = cmul(pw_re, pw_im, a_re, a_im)
    for j in groups:
        u = u_ref[j]
        if carried:
            pre_re, pre_im = h0re_ref[j], h0im_ref[j]
        elif n_tiles == 1:
            pre_re, pre_im = ere_ref[j], eim_ref[j]
        else:
            pre_re = jnp.concatenate([ere_ref[j], sre_ref[j, :, :nc - tl]], axis=1)
            pre_im = jnp.concatenate([eim_ref[j], sim_ref[j, :, :nc - tl]], axis=1)
        prev = jnp.concatenate([pre_re, pre_im], axis=0).astype(BF16)
        y = (jnp.dot(m_ref[j], u, preferred_element_type=F32)
             + jnp.dot(v_ref[j], prev, preferred_element_type=F32)
             + d_ref[j] * u.astype(F32))
        y_ref[:, SSM_GROUP * j:SSM_GROUP * (j + 1), :] = (
            y.reshape(chunk, SSM_GROUP, nc).astype(BF16))


def _ssm(u2, m, w, v, dre, dim, d_col, h0, chunk, n_tiles, segs):
    g, kl, nc = u2.shape
    gb, p = GROUP_BLOCK, SSM_STATE
    carried = h0 is not None
    tl = nc // n_tiles
    wspec = lambda shape: pl.BlockSpec((gb,) + shape, lambda a: (a, 0, 0))
    in_specs = [wspec((kl, nc)), wspec((kl, kl)), wspec((2 * p, kl)), wspec((kl, 2 * p)),
                wspec((p, 8)), wspec((p, 8)), wspec((kl, 1))]
    args = [u2, m, w, v, dre, dim, d_col]
    scratch = [pltpu.VMEM((gb, p, nc), F32), pltpu.VMEM((gb, p, nc), F32)]
    if carried:
        assert n_tiles == 1
        in_specs += [wspec((p, nc))] * 2
        args += list(h0)
    else:
        scratch += [pltpu.VMEM((gb, p, tl), F32), pltpu.VMEM((gb, p, tl), F32)]
    hshape = jax.ShapeDtypeStruct((g, p, tl), F32)
    return pl.pallas_call(
        functools.partial(_ssm_kernel, chunk=chunk, n_tiles=n_tiles, segs=segs, carried=carried),
        grid=(g // gb,), in_specs=in_specs,
        out_specs=(pl.BlockSpec((chunk, SSM_GROUP * gb, nc), lambda a: (0, a, 0)),
                   wspec((p, tl)), wspec((p, tl))),
        out_shape=(jax.ShapeDtypeStruct((chunk, D_INNER, nc), BF16), hshape, hshape),
        scratch_shapes=scratch,
        compiler_params=_cparams(("arbitrary",)), name=f"s5_scan_{chunk}")(*args)


POST_PHASES = 2


def _post0_kernel(y_ref, gate_ref, x5_hbm, wglu_ref, bglu_ref, wout_ref, o_ref, xbuf, sem, *scratch,
                  phase_major_out):
    slot = _fetch_phases(x5_hbm, xbuf, sem, per_step=POST_PHASES)
    phases = range(POST_PHASES)
    ys = [y_ref[ph].astype(F32) for ph in phases]
    ys = [0.5 * y * (1.0 + lax.erf(y * math.sqrt(0.5))) for y in ys]
    zs = [jnp.dot(wglu_ref[...], y.astype(BF16), preferred_element_type=F32) + bglu_ref[...] for y in ys]
    ys = [y * _sigmoid(z) * _silu(gate_ref[ph].astype(F32)) for ph, y, z in zip(phases, ys, zs)]
    outs = [jnp.dot(wout_ref[...], y.astype(BF16), preferred_element_type=F32) for y in ys]
    if phase_major_out:
        for ph, o in zip(phases, outs):
            o_ref[ph] = xbuf[slot, ph] + o.T
    else:
        obuf, osem = scratch

        def fill(oslot):
            for ph, o in zip(phases, outs):
                obuf[oslot, ph] = xbuf[slot, ph] + o.T

        _store_phases(o_ref, obuf, osem, fill, per_step=POST_PHASES)


def _post0(y2, gate2, x5, wglu_t, bglu_col, wout_t, phase_major_out):
    nseq, nq, n_tiles, chunk, _ = x5.shape
    tc = nseq * nq
    nc = tc * n_tiles
    ph = POST_PHASES
    scratch = [pltpu.VMEM((2, ph, tc, D_MODEL), F32), pltpu.SemaphoreType.DMA((2,))]
    if phase_major_out:
        out_shape = jax.ShapeDtypeStruct((chunk, nc, D_MODEL), F32)
        out_spec = pl.BlockSpec((ph, tc, D_MODEL), lambda r, s: (s, r, 0))
    else:
        out_shape = jax.ShapeDtypeStruct(x5.shape, F32)
        out_spec = pl.BlockSpec(memory_space=pl.ANY)
        scratch = scratch * 2
    return pl.pallas_call(
        functools.partial(_post0_kernel, phase_major_out=phase_major_out),
        grid=(n_tiles, chunk // ph),
        in_specs=[pl.BlockSpec((ph, D_INNER, tc), lambda r, s: (s, 0, r)),
                  pl.BlockSpec((ph, D_INNER, tc), lambda r, s: (s, 0, r)),
                  pl.BlockSpec(memory_space=pl.ANY),
                  pl.BlockSpec((D_INNER, D_INNER), lambda r, s: (0, 0)),
                  pl.BlockSpec((D_INNER, 1), lambda r, s: (0, 0)),
                  pl.BlockSpec((D_MODEL, D_INNER), lambda r, s: (0, 0))],
        out_specs=out_spec, out_shape=out_shape,
        scratch_shapes=scratch,
        compiler_params=_cparams(("arbitrary", "arbitrary")), name=f"l0_post_{chunk}")(
            y2, gate2, x5, wglu_t, bglu_col, wout_t)


def _layer_norm_act(z, gate, lng, lnb):
    mu = jnp.mean(z, axis=-1, keepdims=True)
    zc = z - mu
    var = jnp.mean(zc * zc, axis=-1, keepdims=True)
    zn = zc * lax.rsqrt(var + EPS) * lng + lnb
    return _silu(zn) * _silu(gate)


CONV_COLS = 256
HIST_CHUNKS = 8


def _conv_taps(lp):
    taps = {}
    for t in range(lp):
        for k in range(CONV_WIDTH):
            o = t + k - CONV_HIST
            delta = (-o + lp - 1) // lp if o < 0 else 0
            taps[t, k] = (o + lp * delta, delta)
    return taps


def _conv_prompt_kernel(x4_hbm, g_ref, w_ref, dww_ref, dwb_ref, lng_ref, lnb_ref, wout_ref, gf_ref,
                        y4_hbm, cst_ref, xn_ref, vbuf_ref, vsh_ref, gate_ref, zc_ref,
                        xbuf, xsem, ybuf, ysem, *, lp, ct, tiles, n_seq):
    j = pl.program_id(1)
    hist = HIST_CHUNKS
    taps = _conv_taps(lp)
    shifted = sorted({sd for sd in taps.values() if sd[1] > 0})
    shift_slot = {sd: i for i, sd in enumerate(shifted)}
    n_blocks = D_INNER // CONV_COLS
    step = pl.program_id(0) * tiles + j
    n_steps = n_seq * tiles
    buf = step % 2

    def tile_copies(hbm, vmem, sem, stp, slt, to_hbm):
        seq, tile = stp // tiles, stp % tiles
        pairs = [(hbm.at[seq, pl.ds(tile * ct, ct), t, :], vmem.at[slt, t]) for t in range(lp)]
        return [pltpu.make_async_copy(v, h, sem.at[slt]) if to_hbm else
                pltpu.make_async_copy(h, v, sem.at[slt]) for h, v in pairs]

    x_copies = functools.partial(tile_copies, x4_hbm, xbuf, xsem, to_hbm=False)
    y_copies = functools.partial(tile_copies, y4_hbm, ybuf, ysem, to_hbm=True)

    @pl.when(step == 0)
    def _():
        for c in x_copies(0, 0):
            c.start()

    @pl.when(step + 1 < n_steps)
    def _():
        for c in x_copies(step + 1, 1 - buf):
            c.start()

    for c in x_copies(step, buf):
        c.wait()

    @pl.when(j == 0)
    def _():
        vbuf_ref[:, 0:hist, :] = jnp.zeros((lp, hist, D_INNER), F32)

    g = g_ref[...]
    for s in range(lp):
        xn_ref[ct * s:ct * (s + 1), :] = _rms_norm(xbuf[buf, s], g).astype(BF16)

    def lanes(start):
        return pl.ds(pl.multiple_of(start, CONV_COLS), CONV_COLS)

    def project(c):
        xn = xn_ref[...]
        a = jnp.dot(xn, w_ref[:, lanes(c * CONV_COLS)], preferred_element_type=F32)
        b = jnp.dot(xn, w_ref[:, lanes(D_INNER + c * CONV_COLS)], preferred_element_type=F32)
        cols = lanes(c * CONV_COLS)
        gate_ref[:, cols] = jnp.dot(xn, w_ref[:, lanes(2 * D_INNER + c * CONV_COLS)],
                                    preferred_element_type=F32)
        v = a * _sigmoid(b)
        for s in range(lp):
            vbuf_ref[s, hist:hist + ct, cols] = v[ct * s:ct * (s + 1)]
        for (s, delta), i in shift_slot.items():
            vsh_ref[i, :, cols] = vbuf_ref[s, hist - delta:hist - delta + ct, cols]

    def conv(c):
        cols = lanes(c * CONV_COLS)
        for t in range(lp):
            acc = jnp.broadcast_to(dwb_ref[:, cols], (ct, CONV_COLS))
            for k in range(CONV_WIDTH):
                s, delta = taps[t, k]
                slab = vsh_ref[shift_slot[s, delta], :, cols] if delta else vbuf_ref[s, hist:hist + ct, cols]
                acc = acc + dww_ref[k:k + 1, cols] * slab
            zc_ref[ct * t:ct * (t + 1), cols] = acc

    project(0)

    def block(c, carry):
        conv(c)
        project(c + 1)
        return carry

    lax.fori_loop(0, n_blocks - 1, block, 0)
    conv(n_blocks - 1)
    zact = _layer_norm_act(zc_ref[...], gate_ref[...], lng_ref[...], lnb_ref[...]).astype(BF16)
    o = jnp.dot(zact, wout_ref[...], preferred_element_type=F32)
    gf = gf_ref[...]

    @pl.when(step >= 2)
    def _():
        for c in y_copies(step - 2, buf):
            c.wait()

    for t in range(lp):
        ybuf[buf, t] = _rms_norm(xbuf[buf, t] + o[ct * t:ct * (t + 1)], gf)
    for c in y_copies(step, buf):
        c.start()

    @pl.when(step == n_steps - 1)
    def _():
        if n_steps >= 2:
            for c in y_copies(step - 1, 1 - buf):
                c.wait()
        for c in y_copies(step, buf):
            c.wait()

    @pl.when(j == tiles - 1)
    def _():
        for i in range(CONV_HIST):
            tok = lp * ct - CONV_HIST + i
            cst_ref[i:i + 1, :] = vbuf_ref[tok % lp, hist + tok // lp:hist + tok // lp + 1, :]

    vbuf_ref[:, 0:hist, :] = vbuf_ref[:, ct:ct + hist, :]


def _layer1_prompt(x4, g, w_in, dw_w, dw_b, ln_g, ln_b, w_out, gf, ct):
    n, cps, lp, _ = x4.shape
    tiles = cps // ct
    n_shift = len({sd for sd in _conv_taps(lp).values() if sd[1] > 0})
    const = lambda shape: pl.BlockSpec(shape, lambda a, j: (0,) * len(shape), pipeline_mode=pl.Buffered(1))
    return pl.pallas_call(
        functools.partial(_conv_prompt_kernel, lp=lp, ct=ct, tiles=tiles, n_seq=n),
        grid=(n, tiles),
        in_specs=[pl.BlockSpec(memory_space=pl.ANY),
                  const((1, D_MODEL)), const((D_MODEL, 3 * D_INNER)),
                  const((CONV_WIDTH, D_INNER)), const((1, D_INNER)),
                  const((1, D_INNER)), const((1, D_INNER)),
                  const((D_INNER, D_MODEL)), const((1, D_MODEL))],
        out_specs=(pl.BlockSpec(memory_space=pl.ANY),
                   pl.BlockSpec((None, CONV_HIST, D_INNER), lambda a, j: (a, 0, 0))),
        out_shape=(jax.ShapeDtypeStruct(x4.shape, F32),
                   jax.ShapeDtypeStruct((n, CONV_HIST, D_INNER), F32)),
        scratch_shapes=[pltpu.VMEM((lp * ct, D_MODEL), BF16),
                        pltpu.VMEM((lp, HIST_CHUNKS + ct, D_INNER), F32),
                        pltpu.VMEM((n_shift, ct, D_INNER), F32),
                        pltpu.VMEM((lp * ct, D_INNER), F32),
                        pltpu.VMEM((lp * ct, D_INNER), F32),
                        pltpu.VMEM((2, lp, ct, D_MODEL), F32), pltpu.SemaphoreType.DMA((2,)),
                        pltpu.VMEM((2, lp, ct, D_MODEL), F32), pltpu.SemaphoreType.DMA((2,))],
        compiler_params=_cparams(("arbitrary", "arbitrary")), name="l1_prompt")(
            x4, g, w_in, dw_w, dw_b, ln_g, ln_b, w_out, gf)


def _inproj1_kernel(x_ref, g_ref, w_ref, v_ref, gate_ref):
    xn = _rms_norm(x_ref[...], g_ref[...]).astype(BF16)
    abg = jnp.dot(xn, w_ref[...], preferred_element_type=F32)
    v_ref[...] = abg[:, :D_INNER] * _sigmoid(abg[:, D_INNER:2 * D_INNER])
    gate_ref[...] = abg[:, 2 * D_INNER:]


def _inproj1(x1, g, w_in, tm):
    r = x1.shape[0]
    const = lambda shape: pl.BlockSpec(shape, lambda i: (0,) * len(shape))
    return pl.pallas_call(
        _inproj1_kernel, grid=(r // tm,),
        in_specs=[pl.BlockSpec((tm, D_MODEL), lambda i: (i, 0)), const((1, D_MODEL)),
                  const((D_MODEL, 3 * D_INNER))],
        out_specs=(pl.BlockSpec((tm, D_INNER), lambda i: (i, 0)),) * 2,
        out_shape=(jax.ShapeDtypeStruct((r, D_INNER), F32),) * 2,
        compiler_params=_cparams(("arbitrary",)), name="l1_inproj_sample")(x1, g, w_in)


SAMPLE_SEQ_TILE = 16
SAMPLE_COLS = 1024


def _conv_sample_kernel(cache_ref, v_ref, gate_ref, dww_ref, dwb_ref, lng_ref, lnb_ref,
                        z_ref, cst_ref, acc_ref, *, steps):
    ns = cache_ref.shape[0]
    for t in range(steps):
        for c0 in range(0, D_INNER, SAMPLE_COLS):
            acc = jnp.broadcast_to(dwb_ref[:, c0:c0 + SAMPLE_COLS], (ns, SAMPLE_COLS))
            for k in range(CONV_WIDTH):
                jrow = t + k
                if jrow < CONV_HIST:
                    src = cache_ref[:, jrow * D_INNER + c0:jrow * D_INNER + c0 + SAMPLE_COLS]
                else:
                    src = v_ref[jrow - CONV_HIST, :, c0:c0 + SAMPLE_COLS]
                acc = acc + dww_ref[k:k + 1, c0:c0 + SAMPLE_COLS] * src
            acc_ref[t, :, c0:c0 + SAMPLE_COLS] = acc
        z_ref[t] = _layer_norm_act(acc_ref[t], gate_ref[t], lng_ref[...], lnb_ref[...]).astype(BF16)
    keep = CONV_HIST - steps
    cst_ref[:, 0:keep * D_INNER] = cache_ref[:, steps * D_INNER:CONV_HIST * D_INNER]
    for t in range(steps):
        cst_ref[:, (keep + t) * D_INNER:(keep + t + 1) * D_INNER] = v_ref[t]


def _conv_sample(cache2d, v3, gate3, dw_w, dw_b, ln_g, ln_b):
    steps, n, _ = v3.shape
    ns = SAMPLE_SEQ_TILE
    const = lambda shape: pl.BlockSpec(shape, lambda i: (0,) * len(shape))
    return pl.pallas_call(
        functools.partial(_conv_sample_kernel, steps=steps), grid=(n // ns,),
        in_specs=[pl.BlockSpec((ns, CONV_HIST * D_INNER), lambda i: (i, 0)),
                  pl.BlockSpec((steps, ns, D_INNER), lambda i: (0, i, 0)),
                  pl.BlockSpec((steps, ns, D_INNER), lambda i: (0, i, 0)),
                  const((CONV_WIDTH, D_INNER)), const((1, D_INNER)), const((1, D_INNER)),
                  const((1, D_INNER))],
        out_specs=(pl.BlockSpec((steps, ns, D_INNER), lambda i: (0, i, 0)),
                   pl.BlockSpec((ns, CONV_HIST * D_INNER), lambda i: (i, 0))),
        out_shape=(jax.ShapeDtypeStruct((steps, n, D_INNER), BF16),
                   jax.ShapeDtypeStruct((n, CONV_HIST * D_INNER), F32)),
        scratch_shapes=[pltpu.VMEM((steps, ns, D_INNER), F32)],
        compiler_params=_cparams(("arbitrary",)), name="l1_conv_sample")(
            cache2d, v3, gate3, dw_w, dw_b, ln_g, ln_b)


def _out1_kernel(z_ref, x_ref, w_ref, g_ref, y_ref):
    x2 = x_ref[...] + jnp.dot(z_ref[...], w_ref[...], preferred_element_type=F32)
    y_ref[...] = _rms_norm(x2, g_ref[...])


def _out1(z, x1, w_out, g, tm):
    r = x1.shape[0]
    phases = x1.shape[1] // D_MODEL
    const = lambda shape: pl.BlockSpec(shape, lambda i, s: (0,) * len(shape))
    return pl.pallas_call(
        _out1_kernel, grid=(r // tm, phases),
        in_specs=[pl.BlockSpec((tm, D_INNER), lambda i, s: (i, s)),
                  pl.BlockSpec((tm, D_MODEL), lambda i, s: (i, s)),
                  const((D_INNER, D_MODEL)), const((1, D_MODEL))],
        out_specs=pl.BlockSpec((tm, D_MODEL), lambda i, s: (i, s)),
        out_shape=jax.ShapeDtypeStruct((r, phases * D_MODEL), F32),
        compiler_params=_cparams(("arbitrary", "arbitrary")), name="l1_out")(z, x1, w_out, g)


def kernel(x_prompt, x_sample, state_ssm_re, state_ssm_im, cache_conv, norm_g, final_norm_g, ssm_w_in, ssm_a_re, ssm_a_im, ssm_log_dt, ssm_b_re, ssm_b_im, ssm_c_re, ssm_c_im, ssm_d, ssm_w_glu, ssm_b_glu, ssm_w_out, conv_w_in, conv_dw_w, conv_dw_b, conv_ln_g, conv_ln_b, conv_w_out):
    n_p, t_p, _ = x_prompt.shape
    n_s, t_s, _ = x_sample.shape
    g, p = SSM_GROUPS, SSM_STATE

    w_in0_t = ssm_w_in[0].T.astype(BF16)
    w_glu_t = ssm_w_glu[0].T.astype(BF16)
    b_glu_col = ssm_b_glu[0].reshape(D_INNER, 1)
    w_out0_t = ssm_w_out[0].T.astype(BF16)
    w_in1 = conv_w_in[0].astype(BF16)
    w_out1 = conv_w_out[0].astype(BF16)
    g0 = norm_g[0].reshape(1, D_MODEL)
    g1 = norm_g[1].reshape(1, D_MODEL)
    gf = final_norm_g.reshape(1, D_MODEL)
    dw_w, dw_b = conv_dw_w[0], conv_dw_b[0].reshape(1, D_INNER)
    ln_g, ln_b = conv_ln_g[0].reshape(1, D_INNER), conv_ln_b[0].reshape(1, D_INNER)

    def layer0(x5, ops, h0, phase_major_out):
        _, segs, n_tiles, chunk, _ = x5.shape
        m, w, v, dre, dim = ops
        d_col = jnp.tile(ssm_d[0].reshape(g, 1, SSM_GROUP), (1, chunk, 1)).reshape(g, chunk * SSM_GROUP, 1)
        u2, gate2 = _inproj0(x5, g0, w_in0_t)
        y2, hre, him = _ssm(u2, m, w, v, dre, dim, d_col, h0, chunk, n_tiles, segs)
        x1 = _post0(y2, gate2, x5, w_glu_t, b_glu_col, w_out0_t, phase_major_out)
        return x1, hre, him

    lp = PROMPT_CHUNK
    cps = t_p // lp
    segs = cps // SCAN_TILES
    assert t_s <= lp and lp % t_s == 0
    ops_p = _ssm_prep(ssm_a_re[0], ssm_a_im[0], ssm_log_dt[0], ssm_b_re[0], ssm_b_im[0],
                      ssm_c_re[0], ssm_c_im[0], lp, cps.bit_length() - 1, t_s)
    x1p, hre_p, him_p = layer0(x_prompt.reshape(n_p, segs, SCAN_TILES, lp, D_MODEL), ops_p, None, False)
    last = lambda h: jnp.transpose(h[:, :, segs - 1::segs], (2, 0, 1))[None]
    ssm_re_p, ssm_im_p = last(hre_p), last(him_p)
    y4, conv_p = _layer1_prompt(x1p.reshape(n_p, cps, lp, D_MODEL), g1, w_in1, dw_w, dw_b, ln_g, ln_b,
                                w_out1, gf, L1_CHUNK_ROWS)
    y_prompt = y4.reshape(n_p, t_p, D_MODEL)

    h0 = (jnp.transpose(state_ssm_re[0], (1, 2, 0)), jnp.transpose(state_ssm_im[0], (1, 2, 0)))
    m_p, w_p, v_p, dre_p, dim_p = ops_p
    kl_s = SSM_GROUP * t_s
    short_decay = lambda d: jnp.broadcast_to(d[:, :, 7:8], d.shape)
    ops_s = (m_p[:, :kl_s, :kl_s], w_p[:, :, SSM_GROUP * lp - kl_s:], v_p[:, :kl_s, :],
             short_decay(dre_p), short_decay(dim_p))
    x1s, hre_s, him_s = layer0(x_sample.reshape(1, n_s, 1, t_s, D_MODEL), ops_s, h0, True)
    ssm_re_s = jnp.transpose(hre_s, (2, 0, 1))[None]
    ssm_im_s = jnp.transpose(him_s, (2, 0, 1))[None]
    x1s = x1s.reshape(t_s * n_s, D_MODEL)
    v_s, gate_s = _inproj1(x1s, g1, w_in1, tm=256)
    zs, conv_s = _conv_sample(cache_conv[0].reshape(n_s, CONV_HIST * D_INNER),
                              v_s.reshape(t_s, n_s, D_INNER), gate_s.reshape(t_s, n_s, D_INNER),
                              dw_w, dw_b, ln_g, ln_b)
    y_s = _out1(zs.reshape(t_s * n_s, D_INNER), x1s, w_out1, gf, tm=t_s * n_s)
    y_sample = jnp.transpose(y_s.reshape(t_s, n_s, D_MODEL), (1, 0, 2))

    return (y_prompt, y_sample, ssm_re_p, ssm_im_p, conv_p[None],
            ssm_re_s, ssm_im_s, conv_s.reshape(1, n_s, CONV_HIST, D_INNER))
```

```python
import functools
import math

import jax
import jax.numpy as jnp
from jax import lax
from jax.experimental import pallas as pl
from jax.experimental.pallas import tpu as pltpu

D_MODEL = 1024
D_INNER = 2048
SSM_GROUP = 16
SSM_GROUPS = D_INNER // SSM_GROUP
SSM_STATE = 64
CONV_WIDTH = 31
CONV_HIST = CONV_WIDTH - 1
EPS = 1e-6

F32 = jnp.float32
BF16 = jnp.bfloat16

LANES = 128
PROMPT_CHUNK = 16
GROUP_BLOCK = 8
SCAN_TILES = 4
L1_CHUNK_ROWS = 32
VMEM_LIMIT = 56 * 1024 * 1024


def _cparams(sem):
    return pltpu.CompilerParams(dimension_semantics=sem, vmem_limit_bytes=VMEM_LIMIT)


def _rms_norm(x, g):
    ms = jnp.mean(x * x, axis=-1, keepdims=True)
    return x * lax.rsqrt(ms + EPS) * g


def _sigmoid(x):
    return 1.0 / (1.0 + jnp.exp(-x))


def _silu(x):
    return x * _sigmoid(x)


def _cpow(lre, lim, k, nbits, shape):
    pr = jnp.ones(shape, F32)
    pi = jnp.zeros(shape, F32)
    sr, si = lre, lim
    for b in range(nbits):
        take = (lax.shift_right_logical(k, b) & 1) == 1
        pr, pi = (jnp.where(take, pr * sr - pi * si, pr), jnp.where(take, pr * si + pi * sr, pi))
        if b + 1 < nbits:
            sr, si = sr * sr - si * si, 2.0 * sr * si
    return pr, pi


def _prep_kernel(are_ref, aim_ref, ldt_ref, bre_t, bim_t, cre, cim, cre_tt, cim_tt,
                 m_ref, w_ref, v_ref, dre_ref, dim_ref, *, chunk, width, n_pow, short):
    kl = SSM_GROUP * chunk
    reps = width // SSM_GROUP
    lane = lax.broadcasted_iota(jnp.int32, (1, width), 1)
    step = lax.shift_right_logical(lane, 4)
    kexp = jnp.maximum(chunk - 1 - step, 0)
    kv = step + 1
    lane8 = lax.broadcasted_iota(jnp.int32, (1, 8), 1)
    dt_all = jnp.exp(ldt_ref[...])
    ar_all = are_ref[...]
    ai_all = aim_ref[...]
    mag = jnp.exp(ar_all * dt_all)
    lre_all = mag * jnp.cos(ai_all * dt_all)
    lim_all = mag * jnp.sin(ai_all * dt_all)
    den = ar_all * ar_all + ai_all * ai_all
    fre_all = ((lre_all - 1.0) * ar_all + lim_all * ai_all) / den
    fim_all = (lim_all * ar_all - (lre_all - 1.0) * ai_all) / den
    for j in range(GROUP_BLOCK):
        lre, lim = lre_all[:, j:j + 1], lim_all[:, j:j + 1]
        fre, fim = fre_all[:, j:j + 1], fim_all[:, j:j + 1]
        bre = jnp.tile(bre_t[j], (1, reps))
        bim = jnp.tile(bim_t[j], (1, reps))
        bbre = fre * bre - fim * bim
        bbim = fre * bim + fim * bre
        pre, pim = _cpow(lre, lim, kexp, (chunk - 1).bit_length(), (SSM_STATE, width))
        wre = pre * bbre - pim * bbim
        wim = pre * bbim + pim * bbre
        w_ref[j, 0:SSM_STATE, :] = wre[:, :kl].astype(BF16)
        w_ref[j, SSM_STATE:2 * SSM_STATE, :] = wim[:, :kl].astype(BF16)
        ktab = (jnp.dot(cre[j], wre, precision=lax.Precision.HIGHEST, preferred_element_type=F32)
                - jnp.dot(cim[j], wim, precision=lax.Precision.HIGHEST, preferred_element_type=F32))
        for t in range(chunk):
            shift = (width - SSM_GROUP * (chunk - 1 - t)) % width
            r = pltpu.roll(ktab, shift, 1) if shift else ktab
            blk = jnp.where(lane < SSM_GROUP * (t + 1), r, 0.0)
            m_ref[j, SSM_GROUP * t:SSM_GROUP * (t + 1), :] = blk[:, :kl].astype(BF16)
        qre, qim = _cpow(lre, lim, kv, (width // SSM_GROUP).bit_length(), (SSM_STATE, width))
        ct_re = jnp.tile(cre_tt[j], (1, reps))
        ct_im = jnp.tile(cim_tt[j], (1, reps))
        v_t = jnp.concatenate([ct_re * qre - ct_im * qim, -(ct_re * qim + ct_im * qre)], axis=0)
        v_ref[j] = v_t.T[:kl, :].astype(BF16)
        sr, si = lre, lim
        dre = jnp.zeros((SSM_STATE, 8), F32)
        dim = jnp.zeros((SSM_STATE, 8), F32)
        for e in range(chunk.bit_length() - 1):
            if (1 << e) == short:
                dre = jnp.where(lane8 == 7, sr, dre)
                dim = jnp.where(lane8 == 7, si, dim)
            sr, si = sr * sr - si * si, 2.0 * sr * si
        for k in range(n_pow):
            dre = jnp.where(lane8 == k, sr, dre)
            dim = jnp.where(lane8 == k, si, dim)
            if k + 1 < n_pow:
                sr, si = sr * sr - si * si, 2.0 * sr * si
        dre_ref[j] = dre
        dim_ref[j] = dim


def _ssm_prep(a_re, a_im, log_dt, b_re, b_im, c_re, c_im, chunk, n_pow, short):
    g, p, h = SSM_GROUPS, SSM_STATE, SSM_GROUP
    kl = h * chunk
    width = max(kl, LANES)
    gb = GROUP_BLOCK
    cols = lambda a: jnp.transpose(a.reshape(g // gb, gb, p), (0, 2, 1))
    args = (cols(a_re), cols(a_im), log_dt.reshape(g // gb, 1, gb), b_re, b_im,
            c_re, c_im, jnp.swapaxes(c_re, 1, 2), jnp.swapaxes(c_im, 1, 2))

    def spec(shape):
        return pl.BlockSpec((gb,) + shape, lambda i: (i, 0, 0))

    def blockwise(shape):
        return pl.BlockSpec((None,) + shape, lambda i: (i, 0, 0))

    in_specs = [blockwise((p, gb)), blockwise((p, gb)), blockwise((1, gb)),
                spec((p, h)), spec((p, h)), spec((h, p)), spec((h, p)),
                spec((p, h)), spec((p, h))]
    out_shape = (jax.ShapeDtypeStruct((g, kl, kl), BF16), jax.ShapeDtypeStruct((g, 2 * p, kl), BF16),
                 jax.ShapeDtypeStruct((g, kl, 2 * p), BF16),
                 jax.ShapeDtypeStruct((g, p, 8), F32), jax.ShapeDtypeStruct((g, p, 8), F32))
    out_specs = (spec((kl, kl)), spec((2 * p, kl)), spec((kl, 2 * p)), spec((p, 8)), spec((p, 8)))
    return pl.pallas_call(
        functools.partial(_prep_kernel, chunk=chunk, width=width, n_pow=n_pow, short=short),
        grid=(g // gb,), in_specs=in_specs, out_specs=out_specs, out_shape=out_shape,
        compiler_params=_cparams(("arbitrary",)), name=f"s5_prep_{chunk}")(*args)


def _phase_copies(x5_hbm, buf, sem, step, slot, *, per_step, to_hbm=False):
    nseq, nq, _, chunk, _ = x5_hbm.shape
    copies = []
    for ph in range(per_step):
        gp = step * per_step + ph
        r, s = gp // chunk, gp % chunk
        for seq in range(nseq):
            hbm = x5_hbm.at[seq, :, r, s, :]
            vmem = buf.at[slot, ph, pl.ds(seq * nq, nq), :]
            copies.append(pltpu.make_async_copy(vmem, hbm, sem.at[slot]) if to_hbm
                          else pltpu.make_async_copy(hbm, vmem, sem.at[slot]))
    return copies


def _grid_step():
    return (pl.program_id(0) * pl.num_programs(1) + pl.program_id(1),
            pl.num_programs(0) * pl.num_programs(1))


def _fetch_phases(x5_hbm, buf, sem, *, per_step):
    step, n_steps = _grid_step()
    slot = step % 2
    copies = functools.partial(_phase_copies, x5_hbm, buf, sem, per_step=per_step)

    @pl.when(step == 0)
    def _():
        for c in copies(0, 0):
            c.start()

    @pl.when(step + 1 < n_steps)
    def _():
        for c in copies(step + 1, 1 - slot):
            c.start()

    for c in copies(step, slot):
        c.wait()
    return slot


def _store_phases(o5_hbm, buf, sem, fill, *, per_step):
    step, n_steps = _grid_step()
    slot = step % 2
    copies = functools.partial(_phase_copies, o5_hbm, buf, sem, per_step=per_step, to_hbm=True)

    @pl.when(step >= 2)
    def _():
        for c in copies(step - 2, slot):
            c.wait()

    fill(slot)
    for c in copies(step, slot):
        c.start()

    @pl.when(step == n_steps - 1)
    def _():
        @pl.when(step >= 1)
        def _():
            for c in copies(step - 1, 1 - slot):
                c.wait()
        for c in copies(step, slot):
            c.wait()


def _inproj0_kernel(x5_hbm, g_ref, w_ref, u_ref, gate_ref, xbuf, sem):
    slot = _fetch_phases(x5_hbm, xbuf, sem, per_step=1)
    xn = _rms_norm(xbuf[slot, 0], g_ref[...]).astype(BF16)
    res = lax.dot_general(w_ref[...], xn, (((1,), (1,)), ((), ())),
                          preferred_element_type=F32)
    u_ref[...] = res[:D_INNER].reshape(SSM_GROUPS, SSM_GROUP, -1).astype(BF16)
    gate_ref[...] = res[D_INNER:].astype(BF16)


def _inproj0(x5, g, w_t):
    nseq, nq, n_tiles, chunk, _ = x5.shape
    tc = nseq * nq
    nc = tc * n_tiles
    return pl.pallas_call(
        _inproj0_kernel,
        grid=(n_tiles, chunk),
        in_specs=[pl.BlockSpec(memory_space=pl.ANY),
                  pl.BlockSpec((1, D_MODEL), lambda i, s: (0, 0)),
                  pl.BlockSpec((2 * D_INNER, D_MODEL), lambda i, s: (0, 0))],
        out_specs=(pl.BlockSpec((SSM_GROUPS, SSM_GROUP, tc), lambda i, s: (0, s, i)),
                   pl.BlockSpec((None, D_INNER, tc), lambda i, s: (s, 0, i))),
        out_shape=(jax.ShapeDtypeStruct((SSM_GROUPS, SSM_GROUP * chunk, nc), BF16),
                   jax.ShapeDtypeStruct((chunk, D_INNER, nc), BF16)),
        scratch_shapes=[pltpu.VMEM((2, 1, tc, D_MODEL), F32), pltpu.SemaphoreType.DMA((2,))],
        compiler_params=_cparams(("arbitrary", "arbitrary")), name=f"l0_inproj_{chunk}")(x5, g, w_t)


def _ssm_kernel(*refs, chunk, n_tiles, segs, carried):
    if carried:
        (u_ref, m_ref, w_ref, v_ref, dre_ref, dim_ref, d_ref, h0re_ref, h0im_ref,
         y_ref, hre_ref, him_ref, sre_ref, sim_ref) = refs
    else:
        (u_ref, m_ref, w_ref, v_ref, dre_ref, dim_ref, d_ref, y_ref, hfin_ref,
         sre_ref, sim_ref, ere_ref, eim_ref, tt_ref) = refs
    nc = u_ref.shape[-1]
    tl = nc // n_tiles
    p = SSM_STATE
    groups = range(GROUP_BLOCK)

    def tile(r):
        return slice(r * tl, (r + 1) * tl)

    def cmul(ar, ai, br, bi):
        return ar * br - ai * bi, ar * bi + ai * br

    for j in groups:
        z = jnp.dot(w_ref[j], u_ref[j], preferred_element_type=F32)
        sre_ref[j] = z[:p]
        sim_ref[j] = z[p:]
    if carried:
        h0re = h0re_ref[...].T
        h0im = h0im_ref[...].T
        for j in groups:
            dr, di = cmul(dre_ref[j][:, 0:1], dim_ref[j][:, 0:1],
                          h0re[p * j:p * (j + 1)], h0im[p * j:p * (j + 1)])
            sre_ref[j] = sre_ref[j] + dr
            sim_ref[j] = sim_ref[j] + di
        hre_ref[...] = sre_ref[...].reshape(GROUP_BLOCK * p, nc).T
        him_ref[...] = sim_ref[...].reshape(GROUP_BLOCK * p, nc).T
    else:
        lane = lax.broadcasted_iota(jnp.int32, (1, tl), 1)
        posq = lane & (segs - 1)
        for r in range(1, n_tiles):
            for j in groups:
                dr, di = cmul(dre_ref[j][:, 0:1], dim_ref[j][:, 0:1],
                              sre_ref[j, :, tile(r - 1)], sim_ref[j, :, tile(r - 1)])
                sre_ref[j, :, tile(r)] = sre_ref[j, :, tile(r)] + dr
                sim_ref[j, :, tile(r)] = sim_ref[j, :, tile(r)] + di
        for j in groups:
            ere_ref[j] = sre_ref[j, :, tile(n_tiles - 1)]
            eim_ref[j] = sim_ref[j, :, tile(n_tiles - 1)]
        col0 = n_tiles.bit_length() - 1
        for k in range(segs.bit_length() - 1):
            keep = posq >= (1 << k)
            for j in groups:
                ar = jnp.where(keep, dre_ref[j][:, col0 + k:col0 + k + 1], 0.0)
                ai = jnp.where(keep, dim_ref[j][:, col0 + k:col0 + k + 1], 0.0)
                t_re, t_im = ere_ref[j], eim_ref[j]
                dr, di = cmul(ar, ai, pltpu.roll(t_re, 1 << k, 1), pltpu.roll(t_im, 1 << k, 1))
                ere_ref[j] = t_re + dr
                eim_ref[j] = t_im + di
        first = posq == 0
        nseq = tl // segs
        for j in groups:
            tt_ref[j] = jnp.concatenate([ere_ref[j], eim_ref[j]], axis=0).T
            hfin_ref[j] = tt_ref[j, pl.ds(segs - 1, nseq, stride=segs), :]
            ere_ref[j] = jnp.where(first, 0.0, pltpu.roll(ere_ref[j], 1, 1))
            eim_ref[j] = jnp.where(first, 0.0, pltpu.roll(eim_ref[j], 1, 1))
        for j in groups:
            a_re, a_im = dre_ref[j][:, 0:1], dim_ref[j][:, 0:1]
            pw_re, pw_im = a_re, a_im
            e_re, e_im = ere_ref[j], eim_ref[j]
            for r in range(n_tiles):
                dr, di = cmul(pw_re, pw_im, e_re, e_im)
                sre_ref[j, :, tile(r)] = sre_ref[j, :, tile(r)] + dr
                sim_ref[j, :, tile(r)] = sim_ref[j, :, tile(r)] + di
                pw_re, pw_im = cmul(pw_re, pw_im, a_re, a_im)
    for j in groups:
        u = u_ref[j]
        if carried:
            pre_re, pre_im = h0re[p * j:p * (j + 1)], h0im[p * j:p * (j + 1)]
        elif n_tiles == 1:
            pre_re, pre_im = ere_ref[j], eim_ref[j]
        else:
            pre_re = jnp.concatenate([ere_ref[j], sre_ref[j, :, :nc - tl]], axis=1)
            pre_im = jnp.concatenate([eim_ref[j], sim_ref[j, :, :nc - tl]], axis=1)
        prev = jnp.concatenate([pre_re, pre_im], axis=0).astype(BF16)
        y = (jnp.dot(m_ref[j], u, preferred_element_type=F32)
             + jnp.dot(v_ref[j], prev, preferred_element_type=F32)
             + d_ref[j] * u.astype(F32))
        y_ref[:, SSM_GROUP * j:SSM_GROUP * (j + 1), :] = (
            y.reshape(chunk, SSM_GROUP, nc).astype(BF16))


def _ssm(u2, m, w, v, dre, dim, d_col, h0, chunk, n_tiles, segs):
    g, kl, nc = u2.shape
    gb, p = GROUP_BLOCK, SSM_STATE
    carried = h0 is not None
    tl = nc // n_tiles
    wspec = lambda shape: pl.BlockSpec((gb,) + shape, lambda a: (a, 0, 0))
    in_specs = [wspec((kl, nc)), wspec((kl, kl)), wspec((2 * p, kl)), wspec((kl, 2 * p)),
                wspec((p, 8)), wspec((p, 8)), wspec((kl, 1))]
    args = [u2, m, w, v, dre, dim, d_col]
    scratch = [pltpu.VMEM((gb, p, nc), F32), pltpu.VMEM((gb, p, nc), F32)]
    y_spec = pl.BlockSpec((chunk, SSM_GROUP * gb, nc), lambda a: (0, a, 0))
    y_shape = jax.ShapeDtypeStruct((chunk, D_INNER, nc), BF16)
    if carried:
        assert n_tiles == 1
        state_spec = pl.BlockSpec((nc, gb * p), lambda a: (0, a))
        in_specs += [state_spec] * 2
        args += list(h0)
        out_specs = (y_spec, state_spec, state_spec)
        out_shape = (y_shape,) + (jax.ShapeDtypeStruct((nc, g * p), F32),) * 2
    else:
        nseq = tl // segs
        scratch += [pltpu.VMEM((gb, p, tl), F32), pltpu.VMEM((gb, p, tl), F32),
                    pltpu.VMEM((gb, tl, 2 * p), F32)]
        out_specs = (y_spec, wspec((nseq, 2 * p)))
        out_shape = (y_shape, jax.ShapeDtypeStruct((g, nseq, 2 * p), F32))
    return pl.pallas_call(
        functools.partial(_ssm_kernel, chunk=chunk, n_tiles=n_tiles, segs=segs, carried=carried),
        grid=(g // gb,), in_specs=in_specs, out_specs=out_specs, out_shape=out_shape,
        scratch_shapes=scratch,
        compiler_params=_cparams(("arbitrary",)), name=f"s5_scan_{chunk}")(*args)


POST_PHASES = 2


def _post0_kernel(y_ref, gate_ref, x5_hbm, wglu_ref, bglu_ref, wout_ref, o_ref, xbuf, sem, *scratch,
                  phase_major_out):
    slot = _fetch_phases(x5_hbm, xbuf, sem, per_step=POST_PHASES)
    phases = range(POST_PHASES)
    ys = [y_ref[ph].astype(F32) for ph in phases]
    ys = [0.5 * y * (1.0 + lax.erf(y * math.sqrt(0.5))) for y in ys]
    zs = [jnp.dot(wglu_ref[...], y.astype(BF16), preferred_element_type=F32) + bglu_ref[...] for y in ys]
    ys = [y * _sigmoid(z) * _silu(gate_ref[ph].astype(F32)) for ph, y, z in zip(phases, ys, zs)]
    outs = [jnp.dot(wout_ref[...], y.astype(BF16), preferred_element_type=F32) for y in ys]
    if phase_major_out:
        for ph, o in zip(phases, outs):
            o_ref[ph] = xbuf[slot, ph] + o.T
    else:
        obuf, osem = scratch

        def fill(oslot):
            for ph, o in zip(phases, outs):
                obuf[oslot, ph] = xbuf[slot, ph] + o.T

        _store_phases(o_ref, obuf, osem, fill, per_step=POST_PHASES)


def _post0(y2, gate2, x5, wglu_t, bglu_col, wout_t, phase_major_out):
    nseq, nq, n_tiles, chunk, _ = x5.shape
    tc = nseq * nq
    nc = tc * n_tiles
    ph = POST_PHASES
    scratch = [pltpu.VMEM((2, ph, tc, D_MODEL), F32), pltpu.SemaphoreType.DMA((2,))]
    if phase_major_out:
        out_shape = jax.ShapeDtypeStruct((chunk, nc, D_MODEL), F32)
        out_spec = pl.BlockSpec((ph, tc, D_MODEL), lambda r, s: (s, r, 0))
    else:
        out_shape = jax.ShapeDtypeStruct(x5.shape, F32)
        out_spec = pl.BlockSpec(memory_space=pl.ANY)
        scratch = scratch * 2
    return pl.pallas_call(
        functools.partial(_post0_kernel, phase_major_out=phase_major_out),
        grid=(n_tiles, chunk // ph),
        in_specs=[pl.BlockSpec((ph, D_INNER, tc), lambda r, s: (s, 0, r)),
                  pl.BlockSpec((ph, D_INNER, tc), lambda r, s: (s, 0, r)),
                  pl.BlockSpec(memory_space=pl.ANY),
                  pl.BlockSpec((D_INNER, D_INNER), lambda r, s: (0, 0)),
                  pl.BlockSpec((D_INNER, 1), lambda r, s: (0, 0)),
                  pl.BlockSpec((D_MODEL, D_INNER), lambda r, s: (0, 0))],
        out_specs=out_spec, out_shape=out_shape,
        scratch_shapes=scratch,
        compiler_params=_cparams(("arbitrary", "arbitrary")), name=f"l0_post_{chunk}")(
            y2, gate2, x5, wglu_t, bglu_col, wout_t)


def _layer_norm_act(z, gate, lng, lnb):
    mu = jnp.mean(z, axis=-1, keepdims=True)
    zc = z - mu
    var = jnp.mean(zc * zc, axis=-1, keepdims=True)
    zn = zc * lax.rsqrt(var + EPS) * lng + lnb
    return _silu(zn) * _silu(gate)


CONV_COLS = 256
HIST_CHUNKS = 8


def _conv_taps(lp):
    taps = {}
    for t in range(lp):
        for k in range(CONV_WIDTH):
            o = t + k - CONV_HIST
            delta = (-o + lp - 1) // lp if o < 0 else 0
            taps[t, k] = (o + lp * delta, delta)
    return taps


def _conv_prompt_kernel(x4_hbm, g_ref, w_ref, dww_ref, dwb_ref, lng_ref, lnb_ref, wout_ref, gf_ref,
                        y4_hbm, cst_ref, xn_ref, vbuf_ref, vsh_ref, gate_ref, zc_ref,
                        xbuf, xsem, ybuf, ysem, *, lp, ct, tiles, n_seq):
    j = pl.program_id(1)
    hist = HIST_CHUNKS
    taps = _conv_taps(lp)
    shifted = sorted({sd for sd in taps.values() if sd[1] > 0})
    shift_slot = {sd: i for i, sd in enumerate(shifted)}
    n_blocks = D_INNER // CONV_COLS
    step = pl.program_id(0) * tiles + j
    n_steps = n_seq * tiles
    buf = step % 2

    def tile_copies(hbm, vmem, sem, stp, slt, to_hbm):
        seq, tile = stp // tiles, stp % tiles
        pairs = [(hbm.at[seq, pl.ds(tile * ct, ct), t, :], vmem.at[slt, t]) for t in range(lp)]
        return [pltpu.make_async_copy(v, h, sem.at[slt]) if to_hbm else
                pltpu.make_async_copy(h, v, sem.at[slt]) for h, v in pairs]

    x_copies = functools.partial(tile_copies, x4_hbm, xbuf, xsem, to_hbm=False)
    y_copies = functools.partial(tile_copies, y4_hbm, ybuf, ysem, to_hbm=True)

    @pl.when(step == 0)
    def _():
        for c in x_copies(0, 0):
            c.start()

    @pl.when(step + 1 < n_steps)
    def _():
        for c in x_copies(step + 1, 1 - buf):
            c.start()

    for c in x_copies(step, buf):
        c.wait()

    @pl.when(j == 0)
    def _():
        vbuf_ref[:, 0:hist, :] = jnp.zeros((lp, hist, D_INNER), F32)

    g = g_ref[...]
    for s in range(lp):
        xn_ref[ct * s:ct * (s + 1), :] = _rms_norm(xbuf[buf, s], g).astype(BF16)

    def lanes(start):
        return pl.ds(pl.multiple_of(start, CONV_COLS), CONV_COLS)

    def project(c):
        xn = xn_ref[...]
        a = jnp.dot(xn, w_ref[:, lanes(c * CONV_COLS)], preferred_element_type=F32)
        b = jnp.dot(xn, w_ref[:, lanes(D_INNER + c * CONV_COLS)], preferred_element_type=F32)
        cols = lanes(c * CONV_COLS)
        gate_ref[:, cols] = jnp.dot(xn, w_ref[:, lanes(2 * D_INNER + c * CONV_COLS)],
                                    preferred_element_type=F32)
        v = a * _sigmoid(b)
        for s in range(lp):
            vbuf_ref[s, hist:hist + ct, cols] = v[ct * s:ct * (s + 1)]
        for (s, delta), i in shift_slot.items():
            vsh_ref[i, :, cols] = vbuf_ref[s, hist - delta:hist - delta + ct, cols]

    def conv(c):
        cols = lanes(c * CONV_COLS)
        for t in range(lp):
            acc = jnp.broadcast_to(dwb_ref[:, cols], (ct, CONV_COLS))
            for k in range(CONV_WIDTH):
                s, delta = taps[t, k]
                slab = vsh_ref[shift_slot[s, delta], :, cols] if delta else vbuf_ref[s, hist:hist + ct, cols]
                acc = acc + dww_ref[k:k + 1, cols] * slab
            zc_ref[ct * t:ct * (t + 1), cols] = acc

    project(0)

    def block(c, carry):
        conv(c)
        project(c + 1)
        return carry

    lax.fori_loop(0, n_blocks - 1, block, 0)
    conv(n_blocks - 1)
    zact = _layer_norm_act(zc_ref[...], gate_ref[...], lng_ref[...], lnb_ref[...]).astype(BF16)
    o = jnp.dot(zact, wout_ref[...], preferred_element_type=F32)
    gf = gf_ref[...]

    @pl.when(step >= 2)
    def _():
        for c in y_copies(step - 2, buf):
            c.wait()

    for t in range(lp):
        ybuf[buf, t] = _rms_norm(xbuf[buf, t] + o[ct * t:ct * (t + 1)], gf)
    for c in y_copies(step, buf):
        c.start()

    @pl.when(step == n_steps - 1)
    def _():
        if n_steps >= 2:
            for c in y_copies(step - 1, 1 - buf):
                c.wait()
        for c in y_copies(step, buf):
            c.wait()

    @pl.when(j == tiles - 1)
    def _():
        for i in range(CONV_HIST):
            tok = lp * ct - CONV_HIST + i
            cst_ref[i:i + 1, :] = vbuf_ref[tok % lp, hist + tok // lp:hist + tok // lp + 1, :]

    vbuf_ref[:, 0:hist, :] = vbuf_ref[:, ct:ct + hist, :]


def _layer1_prompt(x4, g, w_in, dw_w, dw_b, ln_g, ln_b, w_out, gf, ct):
    n, cps, lp, _ = x4.shape
    tiles = cps // ct
    n_shift = len({sd for sd in _conv_taps(lp).values() if sd[1] > 0})
    const = lambda shape: pl.BlockSpec(shape, lambda a, j: (0,) * len(shape), pipeline_mode=pl.Buffered(1))
    return pl.pallas_call(
        functools.partial(_conv_prompt_kernel, lp=lp, ct=ct, tiles=tiles, n_seq=n),
        grid=(n, tiles),
        in_specs=[pl.BlockSpec(memory_space=pl.ANY),
                  const((1, D_MODEL)), const((D_MODEL, 3 * D_INNER)),
                  const((CONV_WIDTH, D_INNER)), const((1, D_INNER)),
                  const((1, D_INNER)), const((1, D_INNER)),
                  const((D_INNER, D_MODEL)), const((1, D_MODEL))],
        out_specs=(pl.BlockSpec(memory_space=pl.ANY),
                   pl.BlockSpec((None, CONV_HIST, D_INNER), lambda a, j: (a, 0, 0))),
        out_shape=(jax.ShapeDtypeStruct(x4.shape, F32),
                   jax.ShapeDtypeStruct((n, CONV_HIST, D_INNER), F32)),
        scratch_shapes=[pltpu.VMEM((lp * ct, D_MODEL), BF16),
                        pltpu.VMEM((lp, HIST_CHUNKS + ct, D_INNER), F32),
                        pltpu.VMEM((n_shift, ct, D_INNER), F32),
                        pltpu.VMEM((lp * ct, D_INNER), F32),
                        pltpu.VMEM((lp * ct, D_INNER), F32),
                        pltpu.VMEM((2, lp, ct, D_MODEL), F32), pltpu.SemaphoreType.DMA((2,)),
                        pltpu.VMEM((2, lp, ct, D_MODEL), F32), pltpu.SemaphoreType.DMA((2,))],
        compiler_params=_cparams(("arbitrary", "arbitrary")), name="l1_prompt")(
            x4, g, w_in, dw_w, dw_b, ln_g, ln_b, w_out, gf)


def _inproj1_kernel(x_ref, g_ref, w_ref, v_ref, gate_ref):
    xn = _rms_norm(x_ref[...], g_ref[...]).astype(BF16)
    abg = jnp.dot(xn, w_ref[...], preferred_element_type=F32)
    v_ref[...] = abg[:, :D_INNER] * _sigmoid(abg[:, D_INNER:2 * D_INNER])
    gate_ref[...] = abg[:, 2 * D_INNER:]


def _inproj1(x1, g, w_in, tm):
    r = x1.shape[0]
    const = lambda shape: pl.BlockSpec(shape, lambda i: (0,) * len(shape))
    return pl.pallas_call(
        _inproj1_kernel, grid=(r // tm,),
        in_specs=[pl.BlockSpec((tm, D_MODEL), lambda i: (i, 0)), const((1, D_MODEL)),
                  const((D_MODEL, 3 * D_INNER))],
        out_specs=(pl.BlockSpec((tm, D_INNER), lambda i: (i, 0)),) * 2,
        out_shape=(jax.ShapeDtypeStruct((r, D_INNER), F32),) * 2,
        compiler_params=_cparams(("arbitrary",)), name="l1_inproj_sample")(x1, g, w_in)


SAMPLE_SEQ_TILE = 16
SAMPLE_COLS = 1024


def _conv_sample_kernel(cache_ref, v_ref, gate_ref, dww_ref, dwb_ref, lng_ref, lnb_ref,
                        z_ref, cst_ref, acc_ref, *, steps):
    ns = cache_ref.shape[0]
    for t in range(steps):
        for c0 in range(0, D_INNER, SAMPLE_COLS):
            acc = jnp.broadcast_to(dwb_ref[:, c0:c0 + SAMPLE_COLS], (ns, SAMPLE_COLS))
            for k in range(CONV_WIDTH):
                jrow = t + k
                if jrow < CONV_HIST:
                    src = cache_ref[:, jrow * D_INNER + c0:jrow * D_INNER + c0 + SAMPLE_COLS]
                else:
                    src = v_ref[jrow - CONV_HIST, :, c0:c0 + SAMPLE_COLS]
                acc = acc + dww_ref[k:k + 1, c0:c0 + SAMPLE_COLS] * src
            acc_ref[t, :, c0:c0 + SAMPLE_COLS] = acc
        z_ref[t] = _layer_norm_act(acc_ref[t], gate_ref[t], lng_ref[...], lnb_ref[...]).astype(BF16)
    keep = CONV_HIST - steps
    cst_ref[:, 0:keep * D_INNER] = cache_ref[:, steps * D_INNER:CONV_HIST * D_INNER]
    for t in range(steps):
        cst_ref[:, (keep + t) * D_INNER:(keep + t + 1) * D_INNER] = v_ref[t]


def _conv_sample(cache2d, v3, gate3, dw_w, dw_b, ln_g, ln_b):
    steps, n, _ = v3.shape
    ns = SAMPLE_SEQ_TILE
    const = lambda shape: pl.BlockSpec(shape, lambda i: (0,) * len(shape))
    return pl.pallas_call(
        functools.partial(_conv_sample_kernel, steps=steps), grid=(n // ns,),
        in_specs=[pl.BlockSpec((ns, CONV_HIST * D_INNER), lambda i: (i, 0)),
                  pl.BlockSpec((steps, ns, D_INNER), lambda i: (0, i, 0)),
                  pl.BlockSpec((steps, ns, D_INNER), lambda i: (0, i, 0)),
                  const((CONV_WIDTH, D_INNER)), const((1, D_INNER)), const((1, D_INNER)),
                  const((1, D_INNER))],
        out_specs=(pl.BlockSpec((steps, ns, D_INNER), lambda i: (0, i, 0)),
                   pl.BlockSpec((ns, CONV_HIST * D_INNER), lambda i: (i, 0))),
        out_shape=(jax.ShapeDtypeStruct((steps, n, D_INNER), BF16),
                   jax.ShapeDtypeStruct((n, CONV_HIST * D_INNER), F32)),
        scratch_shapes=[pltpu.VMEM((steps, ns, D_INNER), F32)],
        compiler_params=_cparams(("arbitrary",)), name="l1_conv_sample")(
            cache2d, v3, gate3, dw_w, dw_b, ln_g, ln_b)


def _out1_kernel(z_ref, x_ref, w_ref, g_ref, y_ref):
    x2 = x_ref[...] + jnp.dot(z_ref[...], w_ref[...], preferred_element_type=F32)
    y_ref[...] = _rms_norm(x2, g_ref[...])


def _out1(z, x1, w_out, g, tm):
    r = x1.shape[0]
    phases = x1.shape[1] // D_MODEL
    const = lambda shape: pl.BlockSpec(shape, lambda i, s: (0,) * len(shape))
    return pl.pallas_call(
        _out1_kernel, grid=(r // tm, phases),
        in_specs=[pl.BlockSpec((tm, D_INNER), lambda i, s: (i, s)),
                  pl.BlockSpec((tm, D_MODEL), lambda i, s: (i, s)),
                  const((D_INNER, D_MODEL)), const((1, D_MODEL))],
        out_specs=pl.BlockSpec((tm, D_MODEL), lambda i, s: (i, s)),
        out_shape=jax.ShapeDtypeStruct((r, phases * D_MODEL), F32),
        compiler_params=_cparams(("arbitrary", "arbitrary")), name="l1_out")(z, x1, w_out, g)


WEIGHT_BLOCK = 512


def _to_bf16_kernel(w_ref, o_ref, *, transpose):
    w = w_ref[...]
    o_ref[...] = (w.T if transpose else w).astype(BF16)


def _to_bf16(w, transpose):
    r, c = w.shape
    b = WEIGHT_BLOCK
    out_spec = pl.BlockSpec((b, b), (lambda i, j: (j, i)) if transpose else (lambda i, j: (i, j)))
    return pl.pallas_call(
        functools.partial(_to_bf16_kernel, transpose=transpose), grid=(r // b, c // b),
        in_specs=[pl.BlockSpec((b, b), lambda i, j: (i, j))], out_specs=out_spec,
        out_shape=jax.ShapeDtypeStruct((c, r) if transpose else (r, c), BF16),
        compiler_params=_cparams(("arbitrary", "arbitrary")), name="weight_bf16")(w)


def kernel(x_prompt, x_sample, state_ssm_re, state_ssm_im, cache_conv, norm_g, final_norm_g, ssm_w_in, ssm_a_re, ssm_a_im, ssm_log_dt, ssm_b_re, ssm_b_im, ssm_c_re, ssm_c_im, ssm_d, ssm_w_glu, ssm_b_glu, ssm_w_out, conv_w_in, conv_dw_w, conv_dw_b, conv_ln_g, conv_ln_b, conv_w_out):
    n_p, t_p, _ = x_prompt.shape
    n_s, t_s, _ = x_sample.shape
    g, p = SSM_GROUPS, SSM_STATE

    w_in0_t = _to_bf16(ssm_w_in[0], transpose=True)
    w_glu_t = _to_bf16(ssm_w_glu[0], transpose=True)
    b_glu_col = ssm_b_glu[0].reshape(D_INNER, 1)
    w_out0_t = _to_bf16(ssm_w_out[0], transpose=True)
    w_in1 = _to_bf16(conv_w_in[0], transpose=False)
    w_out1 = _to_bf16(conv_w_out[0], transpose=False)
    g0 = norm_g[0].reshape(1, D_MODEL)
    g1 = norm_g[1].reshape(1, D_MODEL)
    gf = final_norm_g.reshape(1, D_MODEL)
    dw_w, dw_b = conv_dw_w[0], conv_dw_b[0].reshape(1, D_INNER)
    ln_g, ln_b = conv_ln_g[0].reshape(1, D_INNER), conv_ln_b[0].reshape(1, D_INNER)

    def layer0(x5, ops, h0, phase_major_out):
        _, segs, n_tiles, chunk, _ = x5.shape
        m, w, v, dre, dim = ops
        d_col = jnp.tile(ssm_d[0].reshape(g, 1, SSM_GROUP), (1, chunk, 1)).reshape(g, chunk * SSM_GROUP, 1)
        u2, gate2 = _inproj0(x5, g0, w_in0_t)
        y2, *states = _ssm(u2, m, w, v, dre, dim, d_col, h0, chunk, n_tiles, segs)
        x1 = _post0(y2, gate2, x5, w_glu_t, b_glu_col, w_out0_t, phase_major_out)
        return x1, states

    lp = PROMPT_CHUNK
    cps = t_p // lp
    segs = cps // SCAN_TILES
    assert t_s <= lp and lp % t_s == 0
    ops_p = _ssm_prep(ssm_a_re[0], ssm_a_im[0], ssm_log_dt[0], ssm_b_re[0], ssm_b_im[0],
                      ssm_c_re[0], ssm_c_im[0], lp, cps.bit_length() - 1, t_s)
    x1p, (hfin,) = layer0(x_prompt.reshape(n_p, segs, SCAN_TILES, lp, D_MODEL), ops_p, None, False)
    hfin = jnp.transpose(hfin, (1, 0, 2))
    ssm_re_p, ssm_im_p = hfin[None, :, :, :p], hfin[None, :, :, p:]
    y4, conv_p = _layer1_prompt(x1p.reshape(n_p, cps, lp, D_MODEL), g1, w_in1, dw_w, dw_b, ln_g, ln_b,
                                w_out1, gf, L1_CHUNK_ROWS)
    y_prompt = y4.reshape(n_p, t_p, D_MODEL)

    h0 = (state_ssm_re[0].reshape(n_s, g * p), state_ssm_im[0].reshape(n_s, g * p))
    m_p, w_p, v_p, dre_p, dim_p = ops_p
    kl_s = SSM_GROUP * t_s
    short_decay = lambda d: jnp.broadcast_to(d[:, :, 7:8], d.shape)
    ops_s = (m_p[:, :kl_s, :kl_s], w_p[:, :, SSM_GROUP * lp - kl_s:], v_p[:, :kl_s, :],
             short_decay(dre_p), short_decay(dim_p))
    x1s, (hre_s, him_s) = layer0(x_sample.reshape(1, n_s, 1, t_s, D_MODEL), ops_s, h0, True)
    ssm_re_s = hre_s.reshape(1, n_s, g, p)
    ssm_im_s = him_s.reshape(1, n_s, g, p)
    x1s = x1s.reshape(t_s * n_s, D_MODEL)
    v_s, gate_s = _inproj1(x1s, g1, w_in1, tm=256)
    zs, conv_s = _conv_sample(cache_conv[0].reshape(n_s, CONV_HIST * D_INNER),
                              v_s.reshape(t_s, n_s, D_INNER), gate_s.reshape(t_s, n_s, D_INNER),
                              dw_w, dw_b, ln_g, ln_b)
    y_s = _out1(zs.reshape(t_s * n_s, D_INNER), x1s, w_out1, gf, tm=t_s * n_s)
    y_sample = jnp.transpose(y_s.reshape(t_s, n_s, D_MODEL), (1, 0, 2))

    return (y_prompt, y_sample, ssm_re_p, ssm_im_p, conv_p[None],
            ssm_re_s, ssm_im_s, conv_s.reshape(1, n_s, CONV_HIST, D_INNER))
```

```python
import functools
import math

import jax
import jax.numpy as jnp
from jax import lax
from jax.experimental import pallas as pl
from jax.experimental.pallas import tpu as pltpu

D_MODEL = 1024
D_INNER = 2048
SSM_GROUP = 16
SSM_GROUPS = D_INNER // SSM_GROUP
SSM_STATE = 64
CONV_WIDTH = 31
CONV_HIST = CONV_WIDTH - 1
EPS = 1e-6

F32 = jnp.float32
BF16 = jnp.bfloat16

LANES = 128
PROMPT_CHUNK = 16
GROUP_BLOCK = 8
SCAN_TILES = 4
L1_CHUNK_ROWS = 32
VMEM_LIMIT = 56 * 1024 * 1024


def _cparams(sem):
    return pltpu.CompilerParams(dimension_semantics=sem, vmem_limit_bytes=VMEM_LIMIT)


def _rms_norm(x, g):
    ms = jnp.mean(x * x, axis=-1, keepdims=True)
    return x * lax.rsqrt(ms + EPS) * g


def _sigmoid(x):
    return 1.0 / (1.0 + jnp.exp(-x))


def _silu(x):
    return x * _sigmoid(x)


def _cpow(lre, lim, k, nbits, shape):
    pr = jnp.ones(shape, F32)
    pi = jnp.zeros(shape, F32)
    sr, si = lre, lim
    for b in range(nbits):
        take = (lax.shift_right_logical(k, b) & 1) == 1
        pr, pi = (jnp.where(take, pr * sr - pi * si, pr), jnp.where(take, pr * si + pi * sr, pi))
        if b + 1 < nbits:
            sr, si = sr * sr - si * si, 2.0 * sr * si
    return pr, pi


def _prep_kernel(are_ref, aim_ref, ldt_ref, bre_t, bim_t, cre, cim, cre_tt, cim_tt,
                 m_ref, w_ref, v_ref, dre_ref, dim_ref, *, chunk, width, n_pow, short):
    kl = SSM_GROUP * chunk
    reps = width // SSM_GROUP
    lane = lax.broadcasted_iota(jnp.int32, (1, width), 1)
    step = lax.shift_right_logical(lane, 4)
    kexp = jnp.maximum(chunk - 1 - step, 0)
    kv = step + 1
    lane8 = lax.broadcasted_iota(jnp.int32, (1, 8), 1)
    dt_all = jnp.exp(ldt_ref[...])
    ar_all = are_ref[...]
    ai_all = aim_ref[...]
    mag = jnp.exp(ar_all * dt_all)
    lre_all = mag * jnp.cos(ai_all * dt_all)
    lim_all = mag * jnp.sin(ai_all * dt_all)
    den = ar_all * ar_all + ai_all * ai_all
    fre_all = ((lre_all - 1.0) * ar_all + lim_all * ai_all) / den
    fim_all = (lim_all * ar_all - (lre_all - 1.0) * ai_all) / den
    for j in range(GROUP_BLOCK):
        lre, lim = lre_all[:, j:j + 1], lim_all[:, j:j + 1]
        fre, fim = fre_all[:, j:j + 1], fim_all[:, j:j + 1]
        bre = jnp.tile(bre_t[j], (1, reps))
        bim = jnp.tile(bim_t[j], (1, reps))
        bbre = fre * bre - fim * bim
        bbim = fre * bim + fim * bre
        pre, pim = _cpow(lre, lim, kexp, (chunk - 1).bit_length(), (SSM_STATE, width))
        wre = pre * bbre - pim * bbim
        wim = pre * bbim + pim * bbre
        w_ref[j, 0:SSM_STATE, :] = wre[:, :kl].astype(BF16)
        w_ref[j, SSM_STATE:2 * SSM_STATE, :] = wim[:, :kl].astype(BF16)
        ktab = (jnp.dot(cre[j], wre, precision=lax.Precision.HIGHEST, preferred_element_type=F32)
                - jnp.dot(cim[j], wim, precision=lax.Precision.HIGHEST, preferred_element_type=F32))
        for t in range(chunk):
            shift = (width - SSM_GROUP * (chunk - 1 - t)) % width
            r = pltpu.roll(ktab, shift, 1) if shift else ktab
            blk = jnp.where(lane < SSM_GROUP * (t + 1), r, 0.0)
            m_ref[j, SSM_GROUP * t:SSM_GROUP * (t + 1), :] = blk[:, :kl].astype(BF16)
        qre, qim = _cpow(lre, lim, kv, (width // SSM_GROUP).bit_length(), (SSM_STATE, width))
        ct_re = jnp.tile(cre_tt[j], (1, reps))
        ct_im = jnp.tile(cim_tt[j], (1, reps))
        v_t = jnp.concatenate([ct_re * qre - ct_im * qim, -(ct_re * qim + ct_im * qre)], axis=0)
        v_ref[j] = v_t.T[:kl, :].astype(BF16)
        sr, si = lre, lim
        dre = jnp.zeros((SSM_STATE, 8), F32)
        dim = jnp.zeros((SSM_STATE, 8), F32)
        for e in range(chunk.bit_length() - 1):
            if (1 << e) == short:
                dre = jnp.where(lane8 == 7, sr, dre)
                dim = jnp.where(lane8 == 7, si, dim)
            sr, si = sr * sr - si * si, 2.0 * sr * si
        for k in range(n_pow):
            dre = jnp.where(lane8 == k, sr, dre)
            dim = jnp.where(lane8 == k, si, dim)
            if k + 1 < n_pow:
                sr, si = sr * sr - si * si, 2.0 * sr * si
        dre_ref[j] = dre
        dim_ref[j] = dim


def _ssm_prep(a_re, a_im, log_dt, b_re, b_im, c_re, c_im, chunk, n_pow, short):
    g, p, h = SSM_GROUPS, SSM_STATE, SSM_GROUP
    kl = h * chunk
    width = max(kl, LANES)
    gb = GROUP_BLOCK
    cols = lambda a: jnp.transpose(a.reshape(g // gb, gb, p), (0, 2, 1))
    args = (cols(a_re), cols(a_im), log_dt.reshape(g // gb, 1, gb), b_re, b_im,
            c_re, c_im, jnp.swapaxes(c_re, 1, 2), jnp.swapaxes(c_im, 1, 2))

    def spec(shape):
        return pl.BlockSpec((gb,) + shape, lambda i: (i, 0, 0))

    def blockwise(shape):
        return pl.BlockSpec((None,) + shape, lambda i: (i, 0, 0))

    in_specs = [blockwise((p, gb)), blockwise((p, gb)), blockwise((1, gb)),
                spec((p, h)), spec((p, h)), spec((h, p)), spec((h, p)),
                spec((p, h)), spec((p, h))]
    out_shape = (jax.ShapeDtypeStruct((g, kl, kl), BF16), jax.ShapeDtypeStruct((g, 2 * p, kl), BF16),
                 jax.ShapeDtypeStruct((g, kl, 2 * p), BF16),
                 jax.ShapeDtypeStruct((g, p, 8), F32), jax.ShapeDtypeStruct((g, p, 8), F32))
    out_specs = (spec((kl, kl)), spec((2 * p, kl)), spec((kl, 2 * p)), spec((p, 8)), spec((p, 8)))
    return pl.pallas_call(
        functools.partial(_prep_kernel, chunk=chunk, width=width, n_pow=n_pow, short=short),
        grid=(g // gb,), in_specs=in_specs, out_specs=out_specs, out_shape=out_shape,
        compiler_params=_cparams(("arbitrary",)), name=f"s5_prep_{chunk}")(*args)


def _phase_copies(x5_hbm, buf, sem, step, slot, *, per_step, to_hbm=False):
    nseq, nq, _, chunk, _ = x5_hbm.shape
    copies = []
    for ph in range(per_step):
        gp = step * per_step + ph
        r, s = gp // chunk, gp % chunk
        for seq in range(nseq):
            hbm = x5_hbm.at[seq, :, r, s, :]
            vmem = buf.at[slot, ph, pl.ds(seq * nq, nq), :]
            copies.append(pltpu.make_async_copy(vmem, hbm, sem.at[slot]) if to_hbm
                          else pltpu.make_async_copy(hbm, vmem, sem.at[slot]))
    return copies


def _grid_step():
    return (pl.program_id(0) * pl.num_programs(1) + pl.program_id(1),
            pl.num_programs(0) * pl.num_programs(1))


def _fetch_phases(x5_hbm, buf, sem, *, per_step):
    step, n_steps = _grid_step()
    slot = step % 2
    copies = functools.partial(_phase_copies, x5_hbm, buf, sem, per_step=per_step)

    @pl.when(step == 0)
    def _():
        for c in copies(0, 0):
            c.start()

    @pl.when(step + 1 < n_steps)
    def _():
        for c in copies(step + 1, 1 - slot):
            c.start()

    for c in copies(step, slot):
        c.wait()
    return slot


def _store_phases(o5_hbm, buf, sem, fill, *, per_step):
    step, n_steps = _grid_step()
    slot = step % 2
    copies = functools.partial(_phase_copies, o5_hbm, buf, sem, per_step=per_step, to_hbm=True)

    @pl.when(step >= 2)
    def _():
        for c in copies(step - 2, slot):
            c.wait()

    fill(slot)
    for c in copies(step, slot):
        c.start()

    @pl.when(step == n_steps - 1)
    def _():
        @pl.when(step >= 1)
        def _():
            for c in copies(step - 1, 1 - slot):
                c.wait()
        for c in copies(step, slot):
            c.wait()


def _inproj0_kernel(x5_hbm, g_ref, w_ref, u_ref, gate_ref, xbuf, sem):
    slot = _fetch_phases(x5_hbm, xbuf, sem, per_step=1)
    xn = _rms_norm(xbuf[slot, 0], g_ref[...]).astype(BF16)
    res = lax.dot_general(w_ref[...], xn, (((1,), (1,)), ((), ())),
                          preferred_element_type=F32)
    u_ref[...] = res[:D_INNER].reshape(SSM_GROUPS, SSM_GROUP, -1).astype(BF16)
    gate_ref[...] = res[D_INNER:].astype(BF16)


def _inproj0(x5, g, w_t):
    nseq, nq, n_tiles, chunk, _ = x5.shape
    tc = nseq * nq
    nc = tc * n_tiles
    return pl.pallas_call(
        _inproj0_kernel,
        grid=(n_tiles, chunk),
        in_specs=[pl.BlockSpec(memory_space=pl.ANY),
                  pl.BlockSpec((1, D_MODEL), lambda i, s: (0, 0)),
                  pl.BlockSpec((2 * D_INNER, D_MODEL), lambda i, s: (0, 0))],
        out_specs=(pl.BlockSpec((SSM_GROUPS, SSM_GROUP, tc), lambda i, s: (0, s, i)),
                   pl.BlockSpec((None, D_INNER, tc), lambda i, s: (s, 0, i))),
        out_shape=(jax.ShapeDtypeStruct((SSM_GROUPS, SSM_GROUP * chunk, nc), BF16),
                   jax.ShapeDtypeStruct((chunk, D_INNER, nc), BF16)),
        scratch_shapes=[pltpu.VMEM((2, 1, tc, D_MODEL), F32), pltpu.SemaphoreType.DMA((2,))],
        compiler_params=_cparams(("arbitrary", "arbitrary")), name=f"l0_inproj_{chunk}")(x5, g, w_t)


def _ssm_kernel(*refs, chunk, n_tiles, segs, carried):
    if carried:
        (u_ref, m_ref, w_ref, v_ref, dre_ref, dim_ref, d_ref, h0re_ref, h0im_ref,
         y_ref, hre_ref, him_ref, sre_ref, sim_ref) = refs
    else:
        (u_ref, m_ref, w_ref, v_ref, dre_ref, dim_ref, d_ref, y_ref, hfin_ref,
         sre_ref, sim_ref, ere_ref, eim_ref, tt_ref) = refs
    nc = u_ref.shape[-1]
    tl = nc // n_tiles
    p = SSM_STATE
    groups = range(GROUP_BLOCK)

    def tile(r):
        return slice(r * tl, (r + 1) * tl)

    def cmul(ar, ai, br, bi):
        return ar * br - ai * bi, ar * bi + ai * br

    for j in groups:
        z = jnp.dot(w_ref[j], u_ref[j], preferred_element_type=F32)
        sre_ref[j] = z[:p]
        sim_ref[j] = z[p:]
    if carried:
        h0re = h0re_ref[...].T
        h0im = h0im_ref[...].T
        for j in groups:
            dr, di = cmul(dre_ref[j][:, 0:1], dim_ref[j][:, 0:1],
                          h0re[p * j:p * (j + 1)], h0im[p * j:p * (j + 1)])
            sre_ref[j] = sre_ref[j] + dr
            sim_ref[j] = sim_ref[j] + di
        hre_ref[...] = sre_ref[...].reshape(GROUP_BLOCK * p, nc).T
        him_ref[...] = sim_ref[...].reshape(GROUP_BLOCK * p, nc).T
    else:
        lane = lax.broadcasted_iota(jnp.int32, (1, tl), 1)
        posq = lane & (segs - 1)
        for r in range(1, n_tiles):
            for j in groups:
                dr, di = cmul(dre_ref[j][:, 0:1], dim_ref[j][:, 0:1],
                              sre_ref[j, :, tile(r - 1)], sim_ref[j, :, tile(r - 1)])
                sre_ref[j, :, tile(r)] = sre_ref[j, :, tile(r)] + dr
                sim_ref[j, :, tile(r)] = sim_ref[j, :, tile(r)] + di
        for j in groups:
            ere_ref[j] = sre_ref[j, :, tile(n_tiles - 1)]
            eim_ref[j] = sim_ref[j, :, tile(n_tiles - 1)]
        col0 = n_tiles.bit_length() - 1
        for k in range(segs.bit_length() - 1):
            keep = posq >= (1 << k)
            for j in groups:
                ar = jnp.where(keep, dre_ref[j][:, col0 + k:col0 + k + 1], 0.0)
                ai = jnp.where(keep, dim_ref[j][:, col0 + k:col0 + k + 1], 0.0)
                t_re, t_im = ere_ref[j], eim_ref[j]
                dr, di = cmul(ar, ai, pltpu.roll(t_re, 1 << k, 1), pltpu.roll(t_im, 1 << k, 1))
                ere_ref[j] = t_re + dr
                eim_ref[j] = t_im + di
        first = posq == 0
        nseq = tl // segs
        for j in groups:
            tt_ref[j] = jnp.concatenate([ere_ref[j], eim_ref[j]], axis=0).T
            hfin_ref[j] = tt_ref[j, pl.ds(segs - 1, nseq, stride=segs), :]
            ere_ref[j] = jnp.where(first, 0.0, pltpu.roll(ere_ref[j], 1, 1))
            eim_ref[j] = jnp.where(first, 0.0, pltpu.roll(eim_ref[j], 1, 1))
        for j in groups:
            a_re, a_im = dre_ref[j][:, 0:1], dim_ref[j][:, 0:1]
            pw_re, pw_im = a_re, a_im
            e_re, e_im = ere_ref[j], eim_ref[j]
            for r in range(n_tiles):
                dr, di = cmul(pw_re, pw_im, e_re, e_im)
                sre_ref[j, :, tile(r)] = sre_ref[j, :, tile(r)] + dr
                sim_ref[j, :, tile(r)] = sim_ref[j, :, tile(r)] + di
                pw_re, pw_im = cmul(pw_re, pw_im, a_re, a_im)
    for j in groups:
        u = u_ref[j]
        if carried:
            pre_re, pre_im = h0re[p * j:p * (j + 1)], h0im[p * j:p * (j + 1)]
        elif n_tiles == 1:
            pre_re, pre_im = ere_ref[j], eim_ref[j]
        else:
            pre_re = jnp.concatenate([ere_ref[j], sre_ref[j, :, :nc - tl]], axis=1)
            pre_im = jnp.concatenate([eim_ref[j], sim_ref[j, :, :nc - tl]], axis=1)
        prev = jnp.concatenate([pre_re, pre_im], axis=0).astype(BF16)
        y = (jnp.dot(m_ref[j], u, preferred_element_type=F32)
             + jnp.dot(v_ref[j], prev, preferred_element_type=F32)
             + d_ref[j] * u.astype(F32))
        y_ref[:, SSM_GROUP * j:SSM_GROUP * (j + 1), :] = (
            y.reshape(chunk, SSM_GROUP, nc).astype(BF16))


def _ssm(u2, m, w, v, dre, dim, d_col, h0, chunk, n_tiles, segs):
    g, kl, nc = u2.shape
    gb, p = GROUP_BLOCK, SSM_STATE
    carried = h0 is not None
    tl = nc // n_tiles
    wspec = lambda shape: pl.BlockSpec((gb,) + shape, lambda a: (a, 0, 0))
    in_specs = [wspec((kl, nc)), wspec((kl, kl)), wspec((2 * p, kl)), wspec((kl, 2 * p)),
                wspec((p, 8)), wspec((p, 8)), wspec((kl, 1))]
    args = [u2, m, w, v, dre, dim, d_col]
    scratch = [pltpu.VMEM((gb, p, nc), F32), pltpu.VMEM((gb, p, nc), F32)]
    y_spec = pl.BlockSpec((chunk, SSM_GROUP * gb, nc), lambda a: (0, a, 0))
    y_shape = jax.ShapeDtypeStruct((chunk, D_INNER, nc), BF16)
    if carried:
        assert n_tiles == 1
        state_spec = pl.BlockSpec((nc, gb * p), lambda a: (0, a))
        in_specs += [state_spec] * 2
        args += list(h0)
        out_specs = (y_spec, state_spec, state_spec)
        out_shape = (y_shape,) + (jax.ShapeDtypeStruct((nc, g * p), F32),) * 2
    else:
        nseq = tl // segs
        scratch += [pltpu.VMEM((gb, p, tl), F32), pltpu.VMEM((gb, p, tl), F32),
                    pltpu.VMEM((gb, tl, 2 * p), F32)]
        out_specs = (y_spec, wspec((nseq, 2 * p)))
        out_shape = (y_shape, jax.ShapeDtypeStruct((g, nseq, 2 * p), F32))
    return pl.pallas_call(
        functools.partial(_ssm_kernel, chunk=chunk, n_tiles=n_tiles, segs=segs, carried=carried),
        grid=(g // gb,), in_specs=in_specs, out_specs=out_specs, out_shape=out_shape,
        scratch_shapes=scratch,
        compiler_params=_cparams(("arbitrary",)), name=f"s5_scan_{chunk}")(*args)


POST_PHASES = 2


def _post0_kernel(y_ref, gate_ref, x5_hbm, wglu_ref, bglu_ref, wout_ref, o_ref, xbuf, sem, *scratch,
                  phase_major_out):
    slot = _fetch_phases(x5_hbm, xbuf, sem, per_step=POST_PHASES)
    phases = range(POST_PHASES)
    ys = [y_ref[ph].astype(F32) for ph in phases]
    ys = [0.5 * y * (1.0 + lax.erf(y * math.sqrt(0.5))) for y in ys]
    zs = [jnp.dot(wglu_ref[...], y.astype(BF16), preferred_element_type=F32) + bglu_ref[...] for y in ys]
    ys = [y * _sigmoid(z) * _silu(gate_ref[ph].astype(F32)) for ph, y, z in zip(phases, ys, zs)]
    outs = [jnp.dot(wout_ref[...], y.astype(BF16), preferred_element_type=F32) for y in ys]
    if phase_major_out:
        for ph, o in zip(phases, outs):
            o_ref[ph] = xbuf[slot, ph] + o.T
    else:
        obuf, osem = scratch

        def fill(oslot):
            for ph, o in zip(phases, outs):
                obuf[oslot, ph] = xbuf[slot, ph] + o.T

        _store_phases(o_ref, obuf, osem, fill, per_step=POST_PHASES)


def _post0(y2, gate2, x5, wglu_t, bglu_col, wout_t, phase_major_out):
    nseq, nq, n_tiles, chunk, _ = x5.shape
    tc = nseq * nq
    nc = tc * n_tiles
    ph = POST_PHASES
    scratch = [pltpu.VMEM((2, ph, tc, D_MODEL), F32), pltpu.SemaphoreType.DMA((2,))]
    if phase_major_out:
        out_shape = jax.ShapeDtypeStruct((chunk, nc, D_MODEL), F32)
        out_spec = pl.BlockSpec((ph, tc, D_MODEL), lambda r, s: (s, r, 0))
    else:
        out_shape = jax.ShapeDtypeStruct(x5.shape, F32)
        out_spec = pl.BlockSpec(memory_space=pl.ANY)
        scratch = scratch * 2
    return pl.pallas_call(
        functools.partial(_post0_kernel, phase_major_out=phase_major_out),
        grid=(n_tiles, chunk // ph),
        in_specs=[pl.BlockSpec((ph, D_INNER, tc), lambda r, s: (s, 0, r)),
                  pl.BlockSpec((ph, D_INNER, tc), lambda r, s: (s, 0, r)),
                  pl.BlockSpec(memory_space=pl.ANY),
                  pl.BlockSpec((D_INNER, D_INNER), lambda r, s: (0, 0)),
                  pl.BlockSpec((D_INNER, 1), lambda r, s: (0, 0)),
                  pl.BlockSpec((D_MODEL, D_INNER), lambda r, s: (0, 0))],
        out_specs=out_spec, out_shape=out_shape,
        scratch_shapes=scratch,
        compiler_params=_cparams(("arbitrary", "arbitrary")), name=f"l0_post_{chunk}")(
            y2, gate2, x5, wglu_t, bglu_col, wout_t)


def _layer_norm_act(z, gate, lng, lnb):
    mu = jnp.mean(z, axis=-1, keepdims=True)
    zc = z - mu
    var = jnp.mean(zc * zc, axis=-1, keepdims=True)
    zn = zc * lax.rsqrt(var + EPS) * lng + lnb
    return _silu(zn) * _silu(gate)


CONV_COLS = 256
HIST_CHUNKS = 8


def _conv_taps(lp):
    taps = {}
    for t in range(lp):
        for k in range(CONV_WIDTH):
            o = t + k - CONV_HIST
            delta = (-o + lp - 1) // lp if o < 0 else 0
            taps[t, k] = (o + lp * delta, delta)
    return taps


def _conv_prompt_kernel(x4_hbm, g_ref, w_ref, dww_ref, dwb_ref, lng_ref, lnb_ref, wout_ref, gf_ref,
                        y4_hbm, cst_ref, xn_ref, vbuf_ref, vsh_ref, gate_ref, zc_ref,
                        xbuf, xsem, ybuf, ysem, *, lp, ct, tiles, n_seq):
    j = pl.program_id(1)
    hist = HIST_CHUNKS
    taps = _conv_taps(lp)
    shifted = sorted({sd for sd in taps.values() if sd[1] > 0})
    shift_slot = {sd: i for i, sd in enumerate(shifted)}
    n_blocks = D_INNER // CONV_COLS
    step = pl.program_id(0) * tiles + j
    n_steps = n_seq * tiles
    buf = step % 2

    def tile_copies(hbm, vmem, sem, stp, slt, to_hbm):
        seq, tile = stp // tiles, stp % tiles
        pairs = [(hbm.at[seq, pl.ds(tile * ct, ct), t, :], vmem.at[slt, t]) for t in range(lp)]
        return [pltpu.make_async_copy(v, h, sem.at[slt]) if to_hbm else
                pltpu.make_async_copy(h, v, sem.at[slt]) for h, v in pairs]

    x_copies = functools.partial(tile_copies, x4_hbm, xbuf, xsem, to_hbm=False)
    y_copies = functools.partial(tile_copies, y4_hbm, ybuf, ysem, to_hbm=True)

    @pl.when(step == 0)
    def _():
        for c in x_copies(0, 0):
            c.start()

    @pl.when(step + 1 < n_steps)
    def _():
        for c in x_copies(step + 1, 1 - buf):
            c.start()

    for c in x_copies(step, buf):
        c.wait()

    @pl.when(j == 0)
    def _():
        vbuf_ref[:, 0:hist, :] = jnp.zeros((lp, hist, D_INNER), F32)

    g = g_ref[...]
    for s in range(lp):
        xn_ref[ct * s:ct * (s + 1), :] = _rms_norm(xbuf[buf, s], g).astype(BF16)

    def lanes(start):
        return pl.ds(pl.multiple_of(start, CONV_COLS), CONV_COLS)

    def project(c):
        xn = xn_ref[...]
        a = jnp.dot(xn, w_ref[:, lanes(c * CONV_COLS)], preferred_element_type=F32)
        b = jnp.dot(xn, w_ref[:, lanes(D_INNER + c * CONV_COLS)], preferred_element_type=F32)
        cols = lanes(c * CONV_COLS)
        gate_ref[:, cols] = jnp.dot(xn, w_ref[:, lanes(2 * D_INNER + c * CONV_COLS)],
                                    preferred_element_type=F32)
        v = a * _sigmoid(b)
        for s in range(lp):
            vbuf_ref[s, hist:hist + ct, cols] = v[ct * s:ct * (s + 1)]
        for (s, delta), i in shift_slot.items():
            vsh_ref[i, :, cols] = vbuf_ref[s, hist - delta:hist - delta + ct, cols]

    def conv(c):
        cols = lanes(c * CONV_COLS)
        for t in range(lp):
            acc = jnp.broadcast_to(dwb_ref[:, cols], (ct, CONV_COLS))
            for k in range(CONV_WIDTH):
                s, delta = taps[t, k]
                slab = vsh_ref[shift_slot[s, delta], :, cols] if delta else vbuf_ref[s, hist:hist + ct, cols]
                acc = acc + dww_ref[k:k + 1, cols] * slab
            zc_ref[ct * t:ct * (t + 1), cols] = acc

    project(0)

    def block(c, carry):
        conv(c)
        project(c + 1)
        return carry

    lax.fori_loop(0, n_blocks - 1, block, 0)
    conv(n_blocks - 1)
    zact = _layer_norm_act(zc_ref[...], gate_ref[...], lng_ref[...], lnb_ref[...]).astype(BF16)
    o = jnp.dot(zact, wout_ref[...], preferred_element_type=F32)
    gf = gf_ref[...]

    @pl.when(step >= 2)
    def _():
        for c in y_copies(step - 2, buf):
            c.wait()

    for t in range(lp):
        ybuf[buf, t] = _rms_norm(xbuf[buf, t] + o[ct * t:ct * (t + 1)], gf)
    for c in y_copies(step, buf):
        c.start()

    @pl.when(step == n_steps - 1)
    def _():
        if n_steps >= 2:
            for c in y_copies(step - 1, 1 - buf):
                c.wait()
        for c in y_copies(step, buf):
            c.wait()

    @pl.when(j == tiles - 1)
    def _():
        for i in range(CONV_HIST):
            tok = lp * ct - CONV_HIST + i
            cst_ref[i:i + 1, :] = vbuf_ref[tok % lp, hist + tok // lp:hist + tok // lp + 1, :]

    vbuf_ref[:, 0:hist, :] = vbuf_ref[:, ct:ct + hist, :]


def _layer1_prompt(x4, g, w_in, dw_w, dw_b, ln_g, ln_b, w_out, gf, ct):
    n, cps, lp, _ = x4.shape
    tiles = cps // ct
    n_shift = len({sd for sd in _conv_taps(lp).values() if sd[1] > 0})
    const = lambda shape: pl.BlockSpec(shape, lambda a, j: (0,) * len(shape), pipeline_mode=pl.Buffered(1))
    return pl.pallas_call(
        functools.partial(_conv_prompt_kernel, lp=lp, ct=ct, tiles=tiles, n_seq=n),
        grid=(n, tiles),
        in_specs=[pl.BlockSpec(memory_space=pl.ANY),
                  const((1, D_MODEL)), const((D_MODEL, 3 * D_INNER)),
                  const((CONV_WIDTH, D_INNER)), const((1, D_INNER)),
                  const((1, D_INNER)), const((1, D_INNER)),
                  const((D_INNER, D_MODEL)), const((1, D_MODEL))],
        out_specs=(pl.BlockSpec(memory_space=pl.ANY),
                   pl.BlockSpec((None, CONV_HIST, D_INNER), lambda a, j: (a, 0, 0))),
        out_shape=(jax.ShapeDtypeStruct(x4.shape, F32),
                   jax.ShapeDtypeStruct((n, CONV_HIST, D_INNER), F32)),
        scratch_shapes=[pltpu.VMEM((lp * ct, D_MODEL), BF16),
                        pltpu.VMEM((lp, HIST_CHUNKS + ct, D_INNER), F32),
                        pltpu.VMEM((n_shift, ct, D_INNER), F32),
                        pltpu.VMEM((lp * ct, D_INNER), F32),
                        pltpu.VMEM((lp * ct, D_INNER), F32),
                        pltpu.VMEM((2, lp, ct, D_MODEL), F32), pltpu.SemaphoreType.DMA((2,)),
                        pltpu.VMEM((2, lp, ct, D_MODEL), F32), pltpu.SemaphoreType.DMA((2,))],
        compiler_params=_cparams(("arbitrary", "arbitrary")), name="l1_prompt")(
            x4, g, w_in, dw_w, dw_b, ln_g, ln_b, w_out, gf)


def _inproj1_kernel(x_ref, g_ref, w_ref, v_ref, gate_ref):
    xn = _rms_norm(x_ref[...], g_ref[...]).astype(BF16)
    abg = jnp.dot(xn, w_ref[...], preferred_element_type=F32)
    v_ref[...] = abg[:, :D_INNER] * _sigmoid(abg[:, D_INNER:2 * D_INNER])
    gate_ref[...] = abg[:, 2 * D_INNER:]


def _inproj1(x1, g, w_in, tm):
    r = x1.shape[0]
    const = lambda shape: pl.BlockSpec(shape, lambda i: (0,) * len(shape))
    return pl.pallas_call(
        _inproj1_kernel, grid=(r // tm,),
        in_specs=[pl.BlockSpec((tm, D_MODEL), lambda i: (i, 0)), const((1, D_MODEL)),
                  const((D_MODEL, 3 * D_INNER))],
        out_specs=(pl.BlockSpec((tm, D_INNER), lambda i: (i, 0)),) * 2,
        out_shape=(jax.ShapeDtypeStruct((r, D_INNER), F32),) * 2,
        compiler_params=_cparams(("arbitrary",)), name="l1_inproj_sample")(x1, g, w_in)


SAMPLE_SEQ_TILE = 16
SAMPLE_COLS = 1024


def _conv_sample_kernel(cache_hbm, v_ref, gate_ref, dww_ref, dwb_ref, lng_ref, lnb_ref,
                        z_ref, newc_hbm, acc_ref, cin, isem, osem, *, steps):
    ns = v_ref.shape[1]
    i = pl.program_id(0)
    slot = i % 2
    keep = CONV_HIST - steps

    def in_copies(stp, slt):
        return [pltpu.make_async_copy(cache_hbm.at[pl.ds(stp * ns, ns), j, :],
                                      cin.at[slt, :, pl.ds(j * D_INNER, D_INNER)], isem.at[slt])
                for j in range(CONV_HIST)]

    @pl.when(i == 0)
    def _():
        for c in in_copies(0, 0):
            c.start()

    @pl.when(i + 1 < pl.num_programs(0))
    def _():
        for c in in_copies(i + 1, 1 - slot):
            c.start()

    seqs = pl.ds(i * ns, ns)
    out_copies = [pltpu.make_async_copy(cache_hbm.at[seqs, steps + r, :], newc_hbm.at[seqs, r, :], osem.at[0])
                  for r in range(keep)]
    out_copies += [pltpu.make_async_copy(v_ref.at[t], newc_hbm.at[seqs, keep + t, :], osem.at[1])
                   for t in range(steps)]
    for c in out_copies:
        c.start()
    for c in in_copies(i, slot):
        c.wait()
    for t in range(steps):
        for c0 in range(0, D_INNER, SAMPLE_COLS):
            acc = jnp.broadcast_to(dwb_ref[:, c0:c0 + SAMPLE_COLS], (ns, SAMPLE_COLS))
            for k in range(CONV_WIDTH):
                jrow = t + k
                if jrow < CONV_HIST:
                    src = cin[slot, :, jrow * D_INNER + c0:jrow * D_INNER + c0 + SAMPLE_COLS]
                else:
                    src = v_ref[jrow - CONV_HIST, :, c0:c0 + SAMPLE_COLS]
                acc = acc + dww_ref[k:k + 1, c0:c0 + SAMPLE_COLS] * src
            acc_ref[t, :, c0:c0 + SAMPLE_COLS] = acc
        z_ref[t] = _layer_norm_act(acc_ref[t], gate_ref[t], lng_ref[...], lnb_ref[...]).astype(BF16)
    for c in out_copies:
        c.wait()


def _conv_sample(cache, v3, gate3, dw_w, dw_b, ln_g, ln_b):
    steps, n, _ = v3.shape
    ns = SAMPLE_SEQ_TILE
    const = lambda shape: pl.BlockSpec(shape, lambda i: (0,) * len(shape))
    return pl.pallas_call(
        functools.partial(_conv_sample_kernel, steps=steps), grid=(n // ns,),
        in_specs=[pl.BlockSpec(memory_space=pl.ANY),
                  pl.BlockSpec((steps, ns, D_INNER), lambda i: (0, i, 0)),
                  pl.BlockSpec((steps, ns, D_INNER), lambda i: (0, i, 0)),
                  const((CONV_WIDTH, D_INNER)), const((1, D_INNER)), const((1, D_INNER)),
                  const((1, D_INNER))],
        out_specs=(pl.BlockSpec((steps, ns, D_INNER), lambda i: (0, i, 0)),
                   pl.BlockSpec(memory_space=pl.ANY)),
        out_shape=(jax.ShapeDtypeStruct((steps, n, D_INNER), BF16),
                   jax.ShapeDtypeStruct(cache.shape, F32)),
        scratch_shapes=[pltpu.VMEM((steps, ns, D_INNER), F32),
                        pltpu.VMEM((2, ns, CONV_HIST * D_INNER), F32),
                        pltpu.SemaphoreType.DMA((2,)), pltpu.SemaphoreType.DMA((2,))],
        compiler_params=_cparams(("arbitrary",)), name="l1_conv_sample")(
            cache, v3, gate3, dw_w, dw_b, ln_g, ln_b)


def _out1_kernel(z_ref, x_ref, w_ref, g_ref, y_ref):
    x2 = x_ref[...] + jnp.dot(z_ref[...], w_ref[...], preferred_element_type=F32)
    y_ref[...] = _rms_norm(x2, g_ref[...])


def _out1(z, x1, w_out, g, tm):
    r = x1.shape[0]
    phases = x1.shape[1] // D_MODEL
    const = lambda shape: pl.BlockSpec(shape, lambda i, s: (0,) * len(shape))
    return pl.pallas_call(
        _out1_kernel, grid=(r // tm, phases),
        in_specs=[pl.BlockSpec((tm, D_INNER), lambda i, s: (i, s)),
                  pl.BlockSpec((tm, D_MODEL), lambda i, s: (i, s)),
                  const((D_INNER, D_MODEL)), const((1, D_MODEL))],
        out_specs=pl.BlockSpec((tm, D_MODEL), lambda i, s: (i, s)),
        out_shape=jax.ShapeDtypeStruct((r, phases * D_MODEL), F32),
        compiler_params=_cparams(("arbitrary", "arbitrary")), name="l1_out")(z, x1, w_out, g)


WEIGHT_BLOCK = 1024


def _to_bf16_kernel(w_ref, o_ref, *, transpose):
    w = w_ref[...]
    o_ref[...] = (w.T if transpose else w).astype(BF16)


def _to_bf16(w, transpose):
    r, c = w.shape
    b = WEIGHT_BLOCK
    out_spec = pl.BlockSpec((b, b), (lambda i, j: (j, i)) if transpose else (lambda i, j: (i, j)))
    return pl.pallas_call(
        functools.partial(_to_bf16_kernel, transpose=transpose), grid=(r // b, c // b),
        in_specs=[pl.BlockSpec((b, b), lambda i, j: (i, j))], out_specs=out_spec,
        out_shape=jax.ShapeDtypeStruct((c, r) if transpose else (r, c), BF16),
        compiler_params=_cparams(("arbitrary", "arbitrary")), name="weight_bf16")(w)


def kernel(x_prompt, x_sample, state_ssm_re, state_ssm_im, cache_conv, norm_g, final_norm_g, ssm_w_in, ssm_a_re, ssm_a_im, ssm_log_dt, ssm_b_re, ssm_b_im, ssm_c_re, ssm_c_im, ssm_d, ssm_w_glu, ssm_b_glu, ssm_w_out, conv_w_in, conv_dw_w, conv_dw_b, conv_ln_g, conv_ln_b, conv_w_out):
    n_p, t_p, _ = x_prompt.shape
    n_s, t_s, _ = x_sample.shape
    g, p = SSM_GROUPS, SSM_STATE

    w_in0_t = _to_bf16(ssm_w_in[0], transpose=True)
    w_glu_t = _to_bf16(ssm_w_glu[0], transpose=True)
    b_glu_col = ssm_b_glu[0].reshape(D_INNER, 1)
    w_out0_t = _to_bf16(ssm_w_out[0], transpose=True)
    w_in1 = _to_bf16(conv_w_in[0], transpose=False)
    w_out1 = _to_bf16(conv_w_out[0], transpose=False)
    g0 = norm_g[0].reshape(1, D_MODEL)
    g1 = norm_g[1].reshape(1, D_MODEL)
    gf = final_norm_g.reshape(1, D_MODEL)
    dw_w, dw_b = conv_dw_w[0], conv_dw_b[0].reshape(1, D_INNER)
    ln_g, ln_b = conv_ln_g[0].reshape(1, D_INNER), conv_ln_b[0].reshape(1, D_INNER)

    def layer0(x5, ops, h0, phase_major_out):
        _, segs, n_tiles, chunk, _ = x5.shape
        m, w, v, dre, dim = ops
        d_col = jnp.tile(ssm_d[0].reshape(g, 1, SSM_GROUP), (1, chunk, 1)).reshape(g, chunk * SSM_GROUP, 1)
        u2, gate2 = _inproj0(x5, g0, w_in0_t)
        y2, *states = _ssm(u2, m, w, v, dre, dim, d_col, h0, chunk, n_tiles, segs)
        x1 = _post0(y2, gate2, x5, w_glu_t, b_glu_col, w_out0_t, phase_major_out)
        return x1, states

    lp = PROMPT_CHUNK
    cps = t_p // lp
    segs = cps // SCAN_TILES
    assert t_s <= lp and lp % t_s == 0
    ops_p = _ssm_prep(ssm_a_re[0], ssm_a_im[0], ssm_log_dt[0], ssm_b_re[0], ssm_b_im[0],
                      ssm_c_re[0], ssm_c_im[0], lp, cps.bit_length() - 1, t_s)
    x1p, (hfin,) = layer0(x_prompt.reshape(n_p, segs, SCAN_TILES, lp, D_MODEL), ops_p, None, False)
    hfin = jnp.transpose(hfin, (1, 0, 2))
    ssm_re_p, ssm_im_p = hfin[None, :, :, :p], hfin[None, :, :, p:]
    y4, conv_p = _layer1_prompt(x1p.reshape(n_p, cps, lp, D_MODEL), g1, w_in1, dw_w, dw_b, ln_g, ln_b,
                                w_out1, gf, L1_CHUNK_ROWS)
    y_prompt = y4.reshape(n_p, t_p, D_MODEL)

    h0 = (state_ssm_re[0].reshape(n_s, g * p), state_ssm_im[0].reshape(n_s, g * p))
    m_p, w_p, v_p, dre_p, dim_p = ops_p
    kl_s = SSM_GROUP * t_s
    short_decay = lambda d: jnp.broadcast_to(d[:, :, 7:8], d.shape)
    ops_s = (m_p[:, :kl_s, :kl_s], w_p[:, :, SSM_GROUP * lp - kl_s:], v_p[:, :kl_s, :],
             short_decay(dre_p), short_decay(dim_p))
    x1s, (hre_s, him_s) = layer0(x_sample.reshape(1, n_s, 1, t_s, D_MODEL), ops_s, h0, True)
    ssm_re_s = hre_s.reshape(1, n_s, g, p)
    ssm_im_s = him_s.reshape(1, n_s, g, p)
    x1s = x1s.reshape(t_s * n_s, D_MODEL)
    v_s, gate_s = _inproj1(x1s, g1, w_in1, tm=256)
    zs, conv_s = _conv_sample(cache_conv[0], v_s.reshape(t_s, n_s, D_INNER), gate_s.reshape(t_s, n_s, D_INNER),
                              dw_w, dw_b, ln_g, ln_b)
    y_s = _out1(zs.reshape(t_s * n_s, D_INNER), x1s, w_out1, gf, tm=t_s * n_s)
    y_sample = jnp.transpose(y_s.reshape(t_s, n_s, D_MODEL), (1, 0, 2))

    return (y_prompt, y_sample, ssm_re_p, ssm_im_p, conv_p[None],
            ssm_re_s, ssm_im_s, conv_s[None])
```

```python
import functools
import math

import jax
import jax.numpy as jnp
from jax import lax
from jax.experimental import pallas as pl
from jax.experimental.pallas import tpu as pltpu

D_MODEL = 1024
D_INNER = 2048
SSM_GROUP = 16
SSM_GROUPS = D_INNER // SSM_GROUP
SSM_STATE = 64
CONV_WIDTH = 31
CONV_HIST = CONV_WIDTH - 1
EPS = 1e-6

F32 = jnp.float32
BF16 = jnp.bfloat16

LANES = 128
PROMPT_CHUNK = 16
GROUP_BLOCK = 8
SCAN_TILES = 4
L1_CHUNK_ROWS = 32
VMEM_LIMIT = 56 * 1024 * 1024


def _cparams(sem):
    return pltpu.CompilerParams(dimension_semantics=sem, vmem_limit_bytes=VMEM_LIMIT)


def _rms_norm(x, g):
    ms = jnp.mean(x * x, axis=-1, keepdims=True)
    return x * lax.rsqrt(ms + EPS) * g


def _sigmoid(x):
    return 1.0 / (1.0 + jnp.exp(-x))


def _silu(x):
    return x * _sigmoid(x)


def _cpow(lre, lim, k, nbits, shape):
    pr = jnp.ones(shape, F32)
    pi = jnp.zeros(shape, F32)
    sr, si = lre, lim
    for b in range(nbits):
        take = (lax.shift_right_logical(k, b) & 1) == 1
        pr, pi = (jnp.where(take, pr * sr - pi * si, pr), jnp.where(take, pr * si + pi * sr, pi))
        if b + 1 < nbits:
            sr, si = sr * sr - si * si, 2.0 * sr * si
    return pr, pi


def _prep_kernel(are_ref, aim_ref, ldt_ref, bre_t, bim_t, cre, cim, cre_tt, cim_tt,
                 m_ref, w_ref, v_ref, dre_ref, dim_ref, *, chunk, width, n_pow, short):
    kl = SSM_GROUP * chunk
    reps = width // SSM_GROUP
    lane = lax.broadcasted_iota(jnp.int32, (1, width), 1)
    step = lax.shift_right_logical(lane, 4)
    kexp = jnp.maximum(chunk - 1 - step, 0)
    kv = step + 1
    lane8 = lax.broadcasted_iota(jnp.int32, (1, 8), 1)
    dt_all = jnp.exp(ldt_ref[...])
    ar_all = are_ref[...]
    ai_all = aim_ref[...]
    mag = jnp.exp(ar_all * dt_all)
    lre_all = mag * jnp.cos(ai_all * dt_all)
    lim_all = mag * jnp.sin(ai_all * dt_all)
    den = ar_all * ar_all + ai_all * ai_all
    fre_all = ((lre_all - 1.0) * ar_all + lim_all * ai_all) / den
    fim_all = (lim_all * ar_all - (lre_all - 1.0) * ai_all) / den
    for j in range(GROUP_BLOCK):
        lre, lim = lre_all[:, j:j + 1], lim_all[:, j:j + 1]
        fre, fim = fre_all[:, j:j + 1], fim_all[:, j:j + 1]
        bre = jnp.tile(bre_t[j], (1, reps))
        bim = jnp.tile(bim_t[j], (1, reps))
        bbre = fre * bre - fim * bim
        bbim = fre * bim + fim * bre
        pre, pim = _cpow(lre, lim, kexp, (chunk - 1).bit_length(), (SSM_STATE, width))
        wre = pre * bbre - pim * bbim
        wim = pre * bbim + pim * bbre
        w_ref[j, 0:SSM_STATE, :] = wre[:, :kl].astype(BF16)
        w_ref[j, SSM_STATE:2 * SSM_STATE, :] = wim[:, :kl].astype(BF16)
        ktab = (jnp.dot(cre[j], wre, precision=lax.Precision.HIGHEST, preferred_element_type=F32)
                - jnp.dot(cim[j], wim, precision=lax.Precision.HIGHEST, preferred_element_type=F32))
        for t in range(chunk):
            shift = (width - SSM_GROUP * (chunk - 1 - t)) % width
            r = pltpu.roll(ktab, shift, 1) if shift else ktab
            blk = jnp.where(lane < SSM_GROUP * (t + 1), r, 0.0)
            m_ref[j, SSM_GROUP * t:SSM_GROUP * (t + 1), :] = blk[:, :kl].astype(BF16)
        qre, qim = _cpow(lre, lim, kv, (width // SSM_GROUP).bit_length(), (SSM_STATE, width))
        ct_re = jnp.tile(cre_tt[j], (1, reps))
        ct_im = jnp.tile(cim_tt[j], (1, reps))
        v_t = jnp.concatenate([ct_re * qre - ct_im * qim, -(ct_re * qim + ct_im * qre)], axis=0)
        v_ref[j] = v_t.T[:kl, :].astype(BF16)
        sr, si = lre, lim
        dre = jnp.zeros((SSM_STATE, 8), F32)
        dim = jnp.zeros((SSM_STATE, 8), F32)
        for e in range(chunk.bit_length() - 1):
            if (1 << e) == short:
                dre = jnp.where(lane8 == 7, sr, dre)
                dim = jnp.where(lane8 == 7, si, dim)
            sr, si = sr * sr - si * si, 2.0 * sr * si
        for k in range(n_pow):
            dre = jnp.where(lane8 == k, sr, dre)
            dim = jnp.where(lane8 == k, si, dim)
            if k + 1 < n_pow:
                sr, si = sr * sr - si * si, 2.0 * sr * si
        dre_ref[j] = dre
        dim_ref[j] = dim


def _ssm_prep(a_re, a_im, log_dt, b_re, b_im, c_re, c_im, chunk, n_pow, short):
    g, p, h = SSM_GROUPS, SSM_STATE, SSM_GROUP
    kl = h * chunk
    width = max(kl, LANES)
    gb = GROUP_BLOCK
    cols = lambda a: jnp.transpose(a.reshape(g // gb, gb, p), (0, 2, 1))
    args = (cols(a_re), cols(a_im), log_dt.reshape(g // gb, 1, gb), b_re, b_im,
            c_re, c_im, jnp.swapaxes(c_re, 1, 2), jnp.swapaxes(c_im, 1, 2))

    def spec(shape):
        return pl.BlockSpec((gb,) + shape, lambda i: (i, 0, 0))

    def blockwise(shape):
        return pl.BlockSpec((None,) + shape, lambda i: (i, 0, 0))

    in_specs = [blockwise((p, gb)), blockwise((p, gb)), blockwise((1, gb)),
                spec((p, h)), spec((p, h)), spec((h, p)), spec((h, p)),
                spec((p, h)), spec((p, h))]
    out_shape = (jax.ShapeDtypeStruct((g, kl, kl), BF16), jax.ShapeDtypeStruct((g, 2 * p, kl), BF16),
                 jax.ShapeDtypeStruct((g, kl, 2 * p), BF16),
                 jax.ShapeDtypeStruct((g, p, 8), F32), jax.ShapeDtypeStruct((g, p, 8), F32))
    out_specs = (spec((kl, kl)), spec((2 * p, kl)), spec((kl, 2 * p)), spec((p, 8)), spec((p, 8)))
    return pl.pallas_call(
        functools.partial(_prep_kernel, chunk=chunk, width=width, n_pow=n_pow, short=short),
        grid=(g // gb,), in_specs=in_specs, out_specs=out_specs, out_shape=out_shape,
        compiler_params=_cparams(("arbitrary",)), name=f"s5_prep_{chunk}")(*args)


def _phase_copies(x5_hbm, buf, sem, step, slot, *, per_step, to_hbm=False):
    nseq, nq, _, chunk, _ = x5_hbm.shape
    copies = []
    for ph in range(per_step):
        gp = step * per_step + ph
        r, s = gp // chunk, gp % chunk
        for seq in range(nseq):
            hbm = x5_hbm.at[seq, :, r, s, :]
            vmem = buf.at[slot, ph, pl.ds(seq * nq, nq), :]
            copies.append(pltpu.make_async_copy(vmem, hbm, sem.at[slot]) if to_hbm
                          else pltpu.make_async_copy(hbm, vmem, sem.at[slot]))
    return copies


def _grid_step():
    return (pl.program_id(0) * pl.num_programs(1) + pl.program_id(1),
            pl.num_programs(0) * pl.num_programs(1))


def _fetch_phases(x5_hbm, buf, sem, *, per_step):
    step, n_steps = _grid_step()
    slot = step % 2
    copies = functools.partial(_phase_copies, x5_hbm, buf, sem, per_step=per_step)

    @pl.when(step == 0)
    def _():
        for c in copies(0, 0):
            c.start()

    @pl.when(step + 1 < n_steps)
    def _():
        for c in copies(step + 1, 1 - slot):
            c.start()

    for c in copies(step, slot):
        c.wait()
    return slot


def _store_phases(o5_hbm, buf, sem, fill, *, per_step):
    step, n_steps = _grid_step()
    slot = step % 2
    copies = functools.partial(_phase_copies, o5_hbm, buf, sem, per_step=per_step, to_hbm=True)

    @pl.when(step >= 2)
    def _():
        for c in copies(step - 2, slot):
            c.wait()

    fill(slot)
    for c in copies(step, slot):
        c.start()

    @pl.when(step == n_steps - 1)
    def _():
        @pl.when(step >= 1)
        def _():
            for c in copies(step - 1, 1 - slot):
                c.wait()
        for c in copies(step, slot):
            c.wait()


def _inproj0_kernel(x5_hbm, g_ref, w_ref, u_ref, gate_ref, xbuf, sem):
    slot = _fetch_phases(x5_hbm, xbuf, sem, per_step=1)
    xn = _rms_norm(xbuf[slot, 0], g_ref[...]).astype(BF16)
    res = lax.dot_general(w_ref[...], xn, (((1,), (1,)), ((), ())),
                          preferred_element_type=F32)
    u_ref[...] = res[:D_INNER].reshape(SSM_GROUPS, SSM_GROUP, -1).astype(BF16)
    gate_ref[...] = res[D_INNER:].astype(BF16)


def _inproj0(x5, g, w_t):
    nseq, nq, n_tiles, chunk, _ = x5.shape
    tc = nseq * nq
    nc = tc * n_tiles
    return pl.pallas_call(
        _inproj0_kernel,
        grid=(n_tiles, chunk),
        in_specs=[pl.BlockSpec(memory_space=pl.ANY),
                  pl.BlockSpec((1, D_MODEL), lambda i, s: (0, 0)),
                  pl.BlockSpec((2 * D_INNER, D_MODEL), lambda i, s: (0, 0))],
        out_specs=(pl.BlockSpec((SSM_GROUPS, SSM_GROUP, tc), lambda i, s: (0, s, i)),
                   pl.BlockSpec((None, D_INNER, tc), lambda i, s: (s, 0, i))),
        out_shape=(jax.ShapeDtypeStruct((SSM_GROUPS, SSM_GROUP * chunk, nc), BF16),
                   jax.ShapeDtypeStruct((chunk, D_INNER, nc), BF16)),
        scratch_shapes=[pltpu.VMEM((2, 1, tc, D_MODEL), F32), pltpu.SemaphoreType.DMA((2,))],
        compiler_params=_cparams(("arbitrary", "arbitrary")), name=f"l0_inproj_{chunk}")(x5, g, w_t)


def _ssm_kernel(*refs, chunk, n_tiles, segs, carried):
    if carried:
        (u_ref, m_ref, w_ref, v_ref, dre_ref, dim_ref, d_ref, h0re_ref, h0im_ref,
         y_ref, hre_ref, him_ref, sre_ref, sim_ref) = refs
    else:
        (u_ref, m_ref, w_ref, v_ref, dre_ref, dim_ref, d_ref, y_ref, hfin_ref,
         sre_ref, sim_ref, ere_ref, eim_ref, tt_ref) = refs
    nc = u_ref.shape[-1]
    tl = nc // n_tiles
    p = SSM_STATE
    groups = range(GROUP_BLOCK)

    def tile(r):
        return slice(r * tl, (r + 1) * tl)

    def cmul(ar, ai, br, bi):
        return ar * br - ai * bi, ar * bi + ai * br

    for j in groups:
        z = jnp.dot(w_ref[j], u_ref[j], preferred_element_type=F32)
        sre_ref[j] = z[:p]
        sim_ref[j] = z[p:]
    if carried:
        h0re = h0re_ref[...].T
        h0im = h0im_ref[...].T
        for j in groups:
            dr, di = cmul(dre_ref[j][:, 0:1], dim_ref[j][:, 0:1],
                          h0re[p * j:p * (j + 1)], h0im[p * j:p * (j + 1)])
            sre_ref[j] = sre_ref[j] + dr
            sim_ref[j] = sim_ref[j] + di
        hre_ref[...] = sre_ref[...].reshape(GROUP_BLOCK * p, nc).T
        him_ref[...] = sim_ref[...].reshape(GROUP_BLOCK * p, nc).T
    else:
        lane = lax.broadcasted_iota(jnp.int32, (1, tl), 1)
        posq = lane & (segs - 1)
        for r in range(1, n_tiles):
            for j in groups:
                dr, di = cmul(dre_ref[j][:, 0:1], dim_ref[j][:, 0:1],
                              sre_ref[j, :, tile(r - 1)], sim_ref[j, :, tile(r - 1)])
                sre_ref[j, :, tile(r)] = sre_ref[j, :, tile(r)] + dr
                sim_ref[j, :, tile(r)] = sim_ref[j, :, tile(r)] + di
        for j in groups:
            ere_ref[j] = sre_ref[j, :, tile(n_tiles - 1)]
            eim_ref[j] = sim_ref[j, :, tile(n_tiles - 1)]
        col0 = n_tiles.bit_length() - 1
        for k in range(segs.bit_length() - 1):
            keep = posq >= (1 << k)
            for j in groups:
                ar = jnp.where(keep, dre_ref[j][:, col0 + k:col0 + k + 1], 0.0)
                ai = jnp.where(keep, dim_ref[j][:, col0 + k:col0 + k + 1], 0.0)
                t_re, t_im = ere_ref[j], eim_ref[j]
                dr, di = cmul(ar, ai, pltpu.roll(t_re, 1 << k, 1), pltpu.roll(t_im, 1 << k, 1))
                ere_ref[j] = t_re + dr
                eim_ref[j] = t_im + di
        first = posq == 0
        nseq = tl // segs
        for j in groups:
            tt_ref[j] = jnp.concatenate([ere_ref[j], eim_ref[j]], axis=0).T
            hfin_ref[j] = tt_ref[j, pl.ds(segs - 1, nseq, stride=segs), :]
            ere_ref[j] = jnp.where(first, 0.0, pltpu.roll(ere_ref[j], 1, 1))
            eim_ref[j] = jnp.where(first, 0.0, pltpu.roll(eim_ref[j], 1, 1))
        for j in groups:
            a_re, a_im = dre_ref[j][:, 0:1], dim_ref[j][:, 0:1]
            pw_re, pw_im = a_re, a_im
            e_re, e_im = ere_ref[j], eim_ref[j]
            for r in range(n_tiles):
                dr, di = cmul(pw_re, pw_im, e_re, e_im)
                sre_ref[j, :, tile(r)] = sre_ref[j, :, tile(r)] + dr
                sim_ref[j, :, tile(r)] = sim_ref[j, :, tile(r)] + di
                pw_re, pw_im = cmul(pw_re, pw_im, a_re, a_im)
    for j in groups:
        u = u_ref[j]
        if carried:
            pre_re, pre_im = h0re[p * j:p * (j + 1)], h0im[p * j:p * (j + 1)]
        elif n_tiles == 1:
            pre_re, pre_im = ere_ref[j], eim_ref[j]
        else:
            pre_re = jnp.concatenate([ere_ref[j], sre_ref[j, :, :nc - tl]], axis=1)
            pre_im = jnp.concatenate([eim_ref[j], sim_ref[j, :, :nc - tl]], axis=1)
        prev = jnp.concatenate([pre_re, pre_im], axis=0).astype(BF16)
        y = (jnp.dot(m_ref[j], u, preferred_element_type=F32)
             + jnp.dot(v_ref[j], prev, preferred_element_type=F32)
             + d_ref[j] * u.astype(F32))
        y_ref[:, SSM_GROUP * j:SSM_GROUP * (j + 1), :] = (
            y.reshape(chunk, SSM_GROUP, nc).astype(BF16))


def _ssm(u2, m, w, v, dre, dim, d_col, h0, chunk, n_tiles, segs):
    g, kl, nc = u2.shape
    gb, p = GROUP_BLOCK, SSM_STATE
    carried = h0 is not None
    tl = nc // n_tiles
    wspec = lambda shape: pl.BlockSpec((gb,) + shape, lambda a: (a, 0, 0))
    in_specs = [wspec((kl, nc)), wspec((kl, kl)), wspec((2 * p, kl)), wspec((kl, 2 * p)),
                wspec((p, 8)), wspec((p, 8)), wspec((kl, 1))]
    args = [u2, m, w, v, dre, dim, d_col]
    scratch = [pltpu.VMEM((gb, p, nc), F32), pltpu.VMEM((gb, p, nc), F32)]
    y_spec = pl.BlockSpec((chunk, SSM_GROUP * gb, nc), lambda a: (0, a, 0))
    y_shape = jax.ShapeDtypeStruct((chunk, D_INNER, nc), BF16)
    if carried:
        assert n_tiles == 1
        state_spec = pl.BlockSpec((nc, gb * p), lambda a: (0, a))
        in_specs += [state_spec] * 2
        args += list(h0)
        out_specs = (y_spec, state_spec, state_spec)
        out_shape = (y_shape,) + (jax.ShapeDtypeStruct((nc, g * p), F32),) * 2
    else:
        nseq = tl // segs
        scratch += [pltpu.VMEM((gb, p, tl), F32), pltpu.VMEM((gb, p, tl), F32),
                    pltpu.VMEM((gb, tl, 2 * p), F32)]
        out_specs = (y_spec, wspec((nseq, 2 * p)))
        out_shape = (y_shape, jax.ShapeDtypeStruct((g, nseq, 2 * p), F32))
    return pl.pallas_call(
        functools.partial(_ssm_kernel, chunk=chunk, n_tiles=n_tiles, segs=segs, carried=carried),
        grid=(g // gb,), in_specs=in_specs, out_specs=out_specs, out_shape=out_shape,
        scratch_shapes=scratch,
        compiler_params=_cparams(("arbitrary",)), name=f"s5_scan_{chunk}")(*args)


POST_PHASES = 2


def _post0_kernel(y_ref, gate_ref, x5_hbm, wglu_ref, bglu_ref, wout_ref, o_ref, xbuf, sem, *scratch,
                  phase_major_out):
    slot = _fetch_phases(x5_hbm, xbuf, sem, per_step=POST_PHASES)
    phases = range(POST_PHASES)
    ys = [y_ref[ph].astype(F32) for ph in phases]
    ys = [0.5 * y * (1.0 + lax.erf(y * math.sqrt(0.5))) for y in ys]
    zs = [jnp.dot(wglu_ref[...], y.astype(BF16), preferred_element_type=F32) + bglu_ref[...] for y in ys]
    ys = [y * _sigmoid(z) * _silu(gate_ref[ph].astype(F32)) for ph, y, z in zip(phases, ys, zs)]
    outs = [jnp.dot(wout_ref[...], y.astype(BF16), preferred_element_type=F32) for y in ys]
    if phase_major_out:
        for ph, o in zip(phases, outs):
            o_ref[ph] = xbuf[slot, ph] + o.T
    else:
        obuf, osem = scratch

        def fill(oslot):
            for ph, o in zip(phases, outs):
                obuf[oslot, ph] = xbuf[slot, ph] + o.T

        _store_phases(o_ref, obuf, osem, fill, per_step=POST_PHASES)


def _post0(y2, gate2, x5, wglu_t, bglu_col, wout_t, phase_major_out):
    nseq, nq, n_tiles, chunk, _ = x5.shape
    tc = nseq * nq
    nc = tc * n_tiles
    ph = POST_PHASES
    scratch = [pltpu.VMEM((2, ph, tc, D_MODEL), F32), pltpu.SemaphoreType.DMA((2,))]
    if phase_major_out:
        out_shape = jax.ShapeDtypeStruct((chunk, nc, D_MODEL), F32)
        out_spec = pl.BlockSpec((ph, tc, D_MODEL), lambda r, s: (s, r, 0))
    else:
        out_shape = jax.ShapeDtypeStruct(x5.shape, F32)
        out_spec = pl.BlockSpec(memory_space=pl.ANY)
        scratch = scratch * 2
    return pl.pallas_call(
        functools.partial(_post0_kernel, phase_major_out=phase_major_out),
        grid=(n_tiles, chunk // ph),
        in_specs=[pl.BlockSpec((ph, D_INNER, tc), lambda r, s: (s, 0, r)),
                  pl.BlockSpec((ph, D_INNER, tc), lambda r, s: (s, 0, r)),
                  pl.BlockSpec(memory_space=pl.ANY),
                  pl.BlockSpec((D_INNER, D_INNER), lambda r, s: (0, 0)),
                  pl.BlockSpec((D_INNER, 1), lambda r, s: (0, 0)),
                  pl.BlockSpec((D_MODEL, D_INNER), lambda r, s: (0, 0))],
        out_specs=out_spec, out_shape=out_shape,
        scratch_shapes=scratch,
        compiler_params=_cparams(("arbitrary", "arbitrary")), name=f"l0_post_{chunk}")(
            y2, gate2, x5, wglu_t, bglu_col, wout_t)


def _layer_norm_act(z, gate, lng, lnb):
    mu = jnp.mean(z, axis=-1, keepdims=True)
    zc = z - mu
    var = jnp.mean(zc * zc, axis=-1, keepdims=True)
    zn = zc * lax.rsqrt(var + EPS) * lng + lnb
    return _silu(zn) * _silu(gate)


CONV_COLS = 256
HIST_CHUNKS = 8


def _conv_taps(lp):
    taps = {}
    for t in range(lp):
        for k in range(CONV_WIDTH):
            o = t + k - CONV_HIST
            delta = (-o + lp - 1) // lp if o < 0 else 0
            taps[t, k] = (o + lp * delta, delta)
    return taps


def _conv_prompt_kernel(x4_hbm, g_ref, w_ref, dww_ref, dwb_ref, lng_ref, lnb_ref, wout_ref, gf_ref,
                        y4_hbm, cst_ref, xn_ref, vbuf_ref, vsh_ref, gate_ref, zc_ref,
                        xbuf, xsem, ybuf, ysem, *, lp, ct, tiles, n_seq):
    j = pl.program_id(1)
    hist = HIST_CHUNKS
    taps = _conv_taps(lp)
    shifted = sorted({sd for sd in taps.values() if sd[1] > 0})
    shift_slot = {sd: i for i, sd in enumerate(shifted)}
    n_blocks = D_INNER // CONV_COLS
    step = pl.program_id(0) * tiles + j
    n_steps = n_seq * tiles
    buf = step % 2

    def tile_copies(hbm, vmem, sem, stp, slt, to_hbm):
        seq, tile = stp // tiles, stp % tiles
        pairs = [(hbm.at[seq, pl.ds(tile * ct, ct), t, :], vmem.at[slt, t]) for t in range(lp)]
        return [pltpu.make_async_copy(v, h, sem.at[slt]) if to_hbm else
                pltpu.make_async_copy(h, v, sem.at[slt]) for h, v in pairs]

    x_copies = functools.partial(tile_copies, x4_hbm, xbuf, xsem, to_hbm=False)
    y_copies = functools.partial(tile_copies, y4_hbm, ybuf, ysem, to_hbm=True)

    @pl.when(step == 0)
    def _():
        for c in x_copies(0, 0):
            c.start()

    @pl.when(step + 1 < n_steps)
    def _():
        for c in x_copies(step + 1, 1 - buf):
            c.start()

    for c in x_copies(step, buf):
        c.wait()

    @pl.when(j == 0)
    def _():
        vbuf_ref[:, 0:hist, :] = jnp.zeros((lp, hist, D_INNER), F32)

    g = g_ref[...]
    for s in range(lp):
        xn_ref[ct * s:ct * (s + 1), :] = _rms_norm(xbuf[buf, s], g).astype(BF16)

    def lanes(start):
        return pl.ds(pl.multiple_of(start, CONV_COLS), CONV_COLS)

    def project(c):
        xn = xn_ref[...]
        a = jnp.dot(xn, w_ref[:, lanes(c * CONV_COLS)], preferred_element_type=F32)
        b = jnp.dot(xn, w_ref[:, lanes(D_INNER + c * CONV_COLS)], preferred_element_type=F32)
        cols = lanes(c * CONV_COLS)
        gate_ref[:, cols] = jnp.dot(xn, w_ref[:, lanes(2 * D_INNER + c * CONV_COLS)],
                                    preferred_element_type=F32)
        v = a * _sigmoid(b)
        for s in range(lp):
            vbuf_ref[s, hist:hist + ct, cols] = v[ct * s:ct * (s + 1)]
        for (s, delta), i in shift_slot.items():
            vsh_ref[i, :, cols] = vbuf_ref[s, hist - delta:hist - delta + ct, cols]

    def conv(c):
        cols = lanes(c * CONV_COLS)
        for t in range(lp):
            acc = jnp.broadcast_to(dwb_ref[:, cols], (ct, CONV_COLS))
            for k in range(CONV_WIDTH):
                s, delta = taps[t, k]
                slab = vsh_ref[shift_slot[s, delta], :, cols] if delta else vbuf_ref[s, hist:hist + ct, cols]
                acc = acc + dww_ref[k:k + 1, cols] * slab
            zc_ref[ct * t:ct * (t + 1), cols] = acc

    project(0)

    def block(c, carry):
        conv(c)
        project(c + 1)
        return carry

    lax.fori_loop(0, n_blocks - 1, block, 0)
    conv(n_blocks - 1)
    zact = _layer_norm_act(zc_ref[...], gate_ref[...], lng_ref[...], lnb_ref[...]).astype(BF16)
    o = jnp.dot(zact, wout_ref[...], preferred_element_type=F32)
    gf = gf_ref[...]

    @pl.when(step >= 2)
    def _():
        for c in y_copies(step - 2, buf):
            c.wait()

    for t in range(lp):
        ybuf[buf, t] = _rms_norm(xbuf[buf, t] + o[ct * t:ct * (t + 1)], gf)
    for c in y_copies(step, buf):
        c.start()

    @pl.when(step == n_steps - 1)
    def _():
        if n_steps >= 2:
            for c in y_copies(step - 1, 1 - buf):
                c.wait()
        for c in y_copies(step, buf):
            c.wait()

    @pl.when(j == tiles - 1)
    def _():
        for i in range(CONV_HIST):
            tok = lp * ct - CONV_HIST + i
            cst_ref[i:i + 1, :] = vbuf_ref[tok % lp, hist + tok // lp:hist + tok // lp + 1, :]

    vbuf_ref[:, 0:hist, :] = vbuf_ref[:, ct:ct + hist, :]


def _layer1_prompt(x4, g, w_in, dw_w, dw_b, ln_g, ln_b, w_out, gf, ct):
    n, cps, lp, _ = x4.shape
    tiles = cps // ct
    n_shift = len({sd for sd in _conv_taps(lp).values() if sd[1] > 0})
    const = lambda shape: pl.BlockSpec(shape, lambda a, j: (0,) * len(shape), pipeline_mode=pl.Buffered(1))
    return pl.pallas_call(
        functools.partial(_conv_prompt_kernel, lp=lp, ct=ct, tiles=tiles, n_seq=n),
        grid=(n, tiles),
        in_specs=[pl.BlockSpec(memory_space=pl.ANY),
                  const((1, D_MODEL)), const((D_MODEL, 3 * D_INNER)),
                  const((CONV_WIDTH, D_INNER)), const((1, D_INNER)),
                  const((1, D_INNER)), const((1, D_INNER)),
                  const((D_INNER, D_MODEL)), const((1, D_MODEL))],
        out_specs=(pl.BlockSpec(memory_space=pl.ANY),
                   pl.BlockSpec((None, CONV_HIST, D_INNER), lambda a, j: (a, 0, 0))),
        out_shape=(jax.ShapeDtypeStruct(x4.shape, F32),
                   jax.ShapeDtypeStruct((n, CONV_HIST, D_INNER), F32)),
        scratch_shapes=[pltpu.VMEM((lp * ct, D_MODEL), BF16),
                        pltpu.VMEM((lp, HIST_CHUNKS + ct, D_INNER), F32),
                        pltpu.VMEM((n_shift, ct, D_INNER), F32),
                        pltpu.VMEM((lp * ct, D_INNER), F32),
                        pltpu.VMEM((lp * ct, D_INNER), F32),
                        pltpu.VMEM((2, lp, ct, D_MODEL), F32), pltpu.SemaphoreType.DMA((2,)),
                        pltpu.VMEM((2, lp, ct, D_MODEL), F32), pltpu.SemaphoreType.DMA((2,))],
        compiler_params=_cparams(("arbitrary", "arbitrary")), name="l1_prompt")(
            x4, g, w_in, dw_w, dw_b, ln_g, ln_b, w_out, gf)


def _inproj1_kernel(x_ref, g_ref, w_ref, v_ref, gate_ref):
    xn = _rms_norm(x_ref[...], g_ref[...]).astype(BF16)
    abg = jnp.dot(xn, w_ref[...], preferred_element_type=F32)
    v_ref[...] = abg[:, :D_INNER] * _sigmoid(abg[:, D_INNER:2 * D_INNER])
    gate_ref[...] = abg[:, 2 * D_INNER:]


def _inproj1(x1, g, w_in, tm):
    r = x1.shape[0]
    const = lambda shape: pl.BlockSpec(shape, lambda i: (0,) * len(shape))
    return pl.pallas_call(
        _inproj1_kernel, grid=(r // tm,),
        in_specs=[pl.BlockSpec((tm, D_MODEL), lambda i: (i, 0)), const((1, D_MODEL)),
                  const((D_MODEL, 3 * D_INNER))],
        out_specs=(pl.BlockSpec((tm, D_INNER), lambda i: (i, 0)),) * 2,
        out_shape=(jax.ShapeDtypeStruct((r, D_INNER), F32),) * 2,
        compiler_params=_cparams(("arbitrary",)), name="l1_inproj_sample")(x1, g, w_in)


SAMPLE_SEQ_TILE = 16
SAMPLE_COLS = 1024


def _conv_sample_kernel(cache_ref, v_ref, gate_ref, dww_ref, dwb_ref, lng_ref, lnb_ref,
                        z_ref, cst_ref, acc_ref, *, steps):
    ns = cache_ref.shape[0]
    for t in range(steps):
        for c0 in range(0, D_INNER, SAMPLE_COLS):
            acc = jnp.broadcast_to(dwb_ref[:, c0:c0 + SAMPLE_COLS], (ns, SAMPLE_COLS))
            for k in range(CONV_WIDTH):
                jrow = t + k
                if jrow < CONV_HIST:
                    src = cache_ref[:, jrow * D_INNER + c0:jrow * D_INNER + c0 + SAMPLE_COLS]
                else:
                    src = v_ref[jrow - CONV_HIST, :, c0:c0 + SAMPLE_COLS]
                acc = acc + dww_ref[k:k + 1, c0:c0 + SAMPLE_COLS] * src
            acc_ref[t, :, c0:c0 + SAMPLE_COLS] = acc
        z_ref[t] = _layer_norm_act(acc_ref[t], gate_ref[t], lng_ref[...], lnb_ref[...]).astype(BF16)
    keep = CONV_HIST - steps
    cst_ref[:, 0:keep * D_INNER] = cache_ref[:, steps * D_INNER:CONV_HIST * D_INNER]
    for t in range(steps):
        cst_ref[:, (keep + t) * D_INNER:(keep + t + 1) * D_INNER] = v_ref[t]


def _conv_sample(cache2d, v3, gate3, dw_w, dw_b, ln_g, ln_b):
    steps, n, _ = v3.shape
    ns = SAMPLE_SEQ_TILE
    const = lambda shape: pl.BlockSpec(shape, lambda i: (0,) * len(shape))
    return pl.pallas_call(
        functools.partial(_conv_sample_kernel, steps=steps), grid=(n // ns,),
        in_specs=[pl.BlockSpec((ns, CONV_HIST * D_INNER), lambda i: (i, 0)),
                  pl.BlockSpec((steps, ns, D_INNER), lambda i: (0, i, 0)),
                  pl.BlockSpec((steps, ns, D_INNER), lambda i: (0, i, 0)),
                  const((CONV_WIDTH, D_INNER)), const((1, D_INNER)), const((1, D_INNER)),
                  const((1, D_INNER))],
        out_specs=(pl.BlockSpec((steps, ns, D_INNER), lambda i: (0, i, 0)),
                   pl.BlockSpec((ns, CONV_HIST * D_INNER), lambda i: (i, 0))),
        out_shape=(jax.ShapeDtypeStruct((steps, n, D_INNER), BF16),
                   jax.ShapeDtypeStruct((n, CONV_HIST * D_INNER), F32)),
        scratch_shapes=[pltpu.VMEM((steps, ns, D_INNER), F32)],
        compiler_params=_cparams(("arbitrary",)), name="l1_conv_sample")(
            cache2d, v3, gate3, dw_w, dw_b, ln_g, ln_b)


def _out1_kernel(z_ref, x_ref, w_ref, g_ref, y_ref):
    x2 = x_ref[...] + jnp.dot(z_ref[...], w_ref[...], preferred_element_type=F32)
    y_ref[...] = _rms_norm(x2, g_ref[...])


def _out1(z, x1, w_out, g, tm):
    r = x1.shape[0]
    phases = x1.shape[1] // D_MODEL
    const = lambda shape: pl.BlockSpec(shape, lambda i, s: (0,) * len(shape))
    return pl.pallas_call(
        _out1_kernel, grid=(r // tm, phases),
        in_specs=[pl.BlockSpec((tm, D_INNER), lambda i, s: (i, s)),
                  pl.BlockSpec((tm, D_MODEL), lambda i, s: (i, s)),
                  const((D_INNER, D_MODEL)), const((1, D_MODEL))],
        out_specs=pl.BlockSpec((tm, D_MODEL), lambda i, s: (i, s)),
        out_shape=jax.ShapeDtypeStruct((r, phases * D_MODEL), F32),
        compiler_params=_cparams(("arbitrary", "arbitrary")), name="l1_out")(z, x1, w_out, g)


WEIGHT_BLOCK = 1024


def _to_bf16_kernel(w_ref, o_ref, *, transpose):
    w = w_ref[...]
    o_ref[...] = (w.T if transpose else w).astype(BF16)


def _to_bf16(w, transpose):
    r, c = w.shape
    b = WEIGHT_BLOCK
    out_spec = pl.BlockSpec((b, b), (lambda i, j: (j, i)) if transpose else (lambda i, j: (i, j)))
    return pl.pallas_call(
        functools.partial(_to_bf16_kernel, transpose=transpose), grid=(r // b, c // b),
        in_specs=[pl.BlockSpec((b, b), lambda i, j: (i, j))], out_specs=out_spec,
        out_shape=jax.ShapeDtypeStruct((c, r) if transpose else (r, c), BF16),
        compiler_params=_cparams(("arbitrary", "arbitrary")), name="weight_bf16")(w)


def kernel(x_prompt, x_sample, state_ssm_re, state_ssm_im, cache_conv, norm_g, final_norm_g, ssm_w_in, ssm_a_re, ssm_a_im, ssm_log_dt, ssm_b_re, ssm_b_im, ssm_c_re, ssm_c_im, ssm_d, ssm_w_glu, ssm_b_glu, ssm_w_out, conv_w_in, conv_dw_w, conv_dw_b, conv_ln_g, conv_ln_b, conv_w_out):
    n_p, t_p, _ = x_prompt.shape
    n_s, t_s, _ = x_sample.shape
    g, p = SSM_GROUPS, SSM_STATE

    w_in0_t = _to_bf16(ssm_w_in[0], transpose=True)
    w_glu_t = _to_bf16(ssm_w_glu[0], transpose=True)
    b_glu_col = ssm_b_glu[0].reshape(D_INNER, 1)
    w_out0_t = _to_bf16(ssm_w_out[0], transpose=True)
    w_in1 = _to_bf16(conv_w_in[0], transpose=False)
    w_out1 = _to_bf16(conv_w_out[0], transpose=False)
    g0 = norm_g[0].reshape(1, D_MODEL)
    g1 = norm_g[1].reshape(1, D_MODEL)
    gf = final_norm_g.reshape(1, D_MODEL)
    dw_w, dw_b = conv_dw_w[0], conv_dw_b[0].reshape(1, D_INNER)
    ln_g, ln_b = conv_ln_g[0].reshape(1, D_INNER), conv_ln_b[0].reshape(1, D_INNER)

    def layer0(x5, ops, h0, phase_major_out):
        _, segs, n_tiles, chunk, _ = x5.shape
        m, w, v, dre, dim = ops
        d_col = jnp.tile(ssm_d[0].reshape(g, 1, SSM_GROUP), (1, chunk, 1)).reshape(g, chunk * SSM_GROUP, 1)
        u2, gate2 = _inproj0(x5, g0, w_in0_t)
        y2, *states = _ssm(u2, m, w, v, dre, dim, d_col, h0, chunk, n_tiles, segs)
        x1 = _post0(y2, gate2, x5, w_glu_t, b_glu_col, w_out0_t, phase_major_out)
        return x1, states

    lp = PROMPT_CHUNK
    cps = t_p // lp
    segs = cps // SCAN_TILES
    assert t_s <= lp and lp % t_s == 0
    ops_p = _ssm_prep(ssm_a_re[0], ssm_a_im[0], ssm_log_dt[0], ssm_b_re[0], ssm_b_im[0],
                      ssm_c_re[0], ssm_c_im[0], lp, cps.bit_length() - 1, t_s)
    x1p, (hfin,) = layer0(x_prompt.reshape(n_p, segs, SCAN_TILES, lp, D_MODEL), ops_p, None, False)
    hfin = jnp.transpose(hfin, (1, 0, 2))
    ssm_re_p, ssm_im_p = hfin[None, :, :, :p], hfin[None, :, :, p:]
    y4, conv_p = _layer1_prompt(x1p.reshape(n_p, cps, lp, D_MODEL), g1, w_in1, dw_w, dw_b, ln_g, ln_b,
                                w_out1, gf, L1_CHUNK_ROWS)
    y_prompt = y4.reshape(n_p, t_p, D_MODEL)

    h0 = (state_ssm_re[0].reshape(n_s, g * p), state_ssm_im[0].reshape(n_s, g * p))
    m_p, w_p, v_p, dre_p, dim_p = ops_p
    kl_s = SSM_GROUP * t_s
    short_decay = lambda d: jnp.broadcast_to(d[:, :, 7:8], d.shape)
    ops_s = (m_p[:, :kl_s, :kl_s], w_p[:, :, SSM_GROUP * lp - kl_s:], v_p[:, :kl_s, :],
             short_decay(dre_p), short_decay(dim_p))
    x1s, (hre_s, him_s) = layer0(x_sample.reshape(1, n_s, 1, t_s, D_MODEL), ops_s, h0, True)
    ssm_re_s = hre_s.reshape(1, n_s, g, p)
    ssm_im_s = him_s.reshape(1, n_s, g, p)
    x1s = x1s.reshape(t_s * n_s, D_MODEL)
    v_s, gate_s = _inproj1(x1s, g1, w_in1, tm=256)
    zs, conv_s = _conv_sample(cache_conv[0].reshape(n_s, CONV_HIST * D_INNER),
                              v_s.reshape(t_s, n_s, D_INNER), gate_s.reshape(t_s, n_s, D_INNER),
                              dw_w, dw_b, ln_g, ln_b)
    y_s = _out1(zs.reshape(t_s * n_s, D_INNER), x1s, w_out1, gf, tm=t_s * n_s)
    y_sample = jnp.transpose(y_s.reshape(t_s, n_s, D_MODEL), (1, 0, 2))

    return (y_prompt, y_sample, ssm_re_p, ssm_im_p, conv_p[None],
            ssm_re_s, ssm_im_s, conv_s.reshape(1, n_s, CONV_HIST, D_INNER))
```

```python
import functools
import math

import jax
import jax.numpy as jnp
from jax import lax
from jax.experimental import pallas as pl
from jax.experimental.pallas import tpu as pltpu

D_MODEL = 1024
D_INNER = 2048
SSM_GROUP = 16
SSM_GROUPS = D_INNER // SSM_GROUP
SSM_STATE = 64
CONV_WIDTH = 31
CONV_HIST = CONV_WIDTH - 1
EPS = 1e-6

F32 = jnp.float32
BF16 = jnp.bfloat16

LANES = 128
PROMPT_CHUNK = 16
GROUP_BLOCK = 8
CARRIED_GROUP_BLOCK = 32
SCAN_TILES = 4
L1_CHUNK_ROWS = 32
VMEM_LIMIT = 56 * 1024 * 1024


def _cparams(sem):
    return pltpu.CompilerParams(dimension_semantics=sem, vmem_limit_bytes=VMEM_LIMIT)


def _rms_norm(x, g):
    ms = jnp.mean(x * x, axis=-1, keepdims=True)
    return x * lax.rsqrt(ms + EPS) * g


def _sigmoid(x):
    return 1.0 / (1.0 + jnp.exp(-x))


def _silu(x):
    return x * _sigmoid(x)


def _cpow(lre, lim, k, nbits, shape):
    pr = jnp.ones(shape, F32)
    pi = jnp.zeros(shape, F32)
    sr, si = lre, lim
    for b in range(nbits):
        take = (lax.shift_right_logical(k, b) & 1) == 1
        pr, pi = (jnp.where(take, pr * sr - pi * si, pr), jnp.where(take, pr * si + pi * sr, pi))
        if b + 1 < nbits:
            sr, si = sr * sr - si * si, 2.0 * sr * si
    return pr, pi


def _prep_kernel(are_ref, aim_ref, ldt_ref, bre_t, bim_t, cre, cim, cre_tt, cim_tt,
                 m_ref, w_ref, v_ref, dre_ref, dim_ref, *, chunk, width, n_pow, short):
    kl = SSM_GROUP * chunk
    reps = width // SSM_GROUP
    lane = lax.broadcasted_iota(jnp.int32, (1, width), 1)
    step = lax.shift_right_logical(lane, 4)
    kexp = jnp.maximum(chunk - 1 - step, 0)
    kv = step + 1
    lane8 = lax.broadcasted_iota(jnp.int32, (1, 8), 1)
    dt_all = jnp.exp(ldt_ref[...])
    ar_all = are_ref[...]
    ai_all = aim_ref[...]
    mag = jnp.exp(ar_all * dt_all)
    lre_all = mag * jnp.cos(ai_all * dt_all)
    lim_all = mag * jnp.sin(ai_all * dt_all)
    den = ar_all * ar_all + ai_all * ai_all
    fre_all = ((lre_all - 1.0) * ar_all + lim_all * ai_all) / den
    fim_all = (lim_all * ar_all - (lre_all - 1.0) * ai_all) / den
    for j in range(GROUP_BLOCK):
        lre, lim = lre_all[:, j:j + 1], lim_all[:, j:j + 1]
        fre, fim = fre_all[:, j:j + 1], fim_all[:, j:j + 1]
        bre = jnp.tile(bre_t[j], (1, reps))
        bim = jnp.tile(bim_t[j], (1, reps))
        bbre = fre * bre - fim * bim
        bbim = fre * bim + fim * bre
        pre, pim = _cpow(lre, lim, kexp, (chunk - 1).bit_length(), (SSM_STATE, width))
        wre = pre * bbre - pim * bbim
        wim = pre * bbim + pim * bbre
        w_ref[j, 0:SSM_STATE, :] = wre[:, :kl].astype(BF16)
        w_ref[j, SSM_STATE:2 * SSM_STATE, :] = wim[:, :kl].astype(BF16)
        ktab = (jnp.dot(cre[j], wre, precision=lax.Precision.HIGHEST, preferred_element_type=F32)
                - jnp.dot(cim[j], wim, precision=lax.Precision.HIGHEST, preferred_element_type=F32))
        for t in range(chunk):
            shift = (width - SSM_GROUP * (chunk - 1 - t)) % width
            r = pltpu.roll(ktab, shift, 1) if shift else ktab
            blk = jnp.where(lane < SSM_GROUP * (t + 1), r, 0.0)
            m_ref[j, SSM_GROUP * t:SSM_GROUP * (t + 1), :] = blk[:, :kl].astype(BF16)
        qre, qim = _cpow(lre, lim, kv, (width // SSM_GROUP).bit_length(), (SSM_STATE, width))
        ct_re = jnp.tile(cre_tt[j], (1, reps))
        ct_im = jnp.tile(cim_tt[j], (1, reps))
        v_t = jnp.concatenate([ct_re * qre - ct_im * qim, -(ct_re * qim + ct_im * qre)], axis=0)
        v_ref[j] = v_t.T[:kl, :].astype(BF16)
        sr, si = lre, lim
        dre = jnp.zeros((SSM_STATE, 8), F32)
        dim = jnp.zeros((SSM_STATE, 8), F32)
        for e in range(chunk.bit_length() - 1):
            if (1 << e) == short:
                dre = jnp.where(lane8 == 7, sr, dre)
                dim = jnp.where(lane8 == 7, si, dim)
            sr, si = sr * sr - si * si, 2.0 * sr * si
        for k in range(n_pow):
            dre = jnp.where(lane8 == k, sr, dre)
            dim = jnp.where(lane8 == k, si, dim)
            if k + 1 < n_pow:
                sr, si = sr * sr - si * si, 2.0 * sr * si
        dre_ref[j] = dre
        dim_ref[j] = dim


def _ssm_prep(a_re, a_im, log_dt, b_re, b_im, c_re, c_im, chunk, n_pow, short):
    g, p, h = SSM_GROUPS, SSM_STATE, SSM_GROUP
    kl = h * chunk
    width = max(kl, LANES)
    gb = GROUP_BLOCK
    cols = lambda a: jnp.transpose(a.reshape(g // gb, gb, p), (0, 2, 1))
    args = (cols(a_re), cols(a_im), log_dt.reshape(g // gb, 1, gb), b_re, b_im,
            c_re, c_im, jnp.swapaxes(c_re, 1, 2), jnp.swapaxes(c_im, 1, 2))

    def spec(shape):
        return pl.BlockSpec((gb,) + shape, lambda i: (i, 0, 0))

    def blockwise(shape):
        return pl.BlockSpec((None,) + shape, lambda i: (i, 0, 0))

    in_specs = [blockwise((p, gb)), blockwise((p, gb)), blockwise((1, gb)),
                spec((p, h)), spec((p, h)), spec((h, p)), spec((h, p)),
                spec((p, h)), spec((p, h))]
    out_shape = (jax.ShapeDtypeStruct((g, kl, kl), BF16), jax.ShapeDtypeStruct((g, 2 * p, kl), BF16),
                 jax.ShapeDtypeStruct((g, kl, 2 * p), BF16),
                 jax.ShapeDtypeStruct((g, p, 8), F32), jax.ShapeDtypeStruct((g, p, 8), F32))
    out_specs = (spec((kl, kl)), spec((2 * p, kl)), spec((kl, 2 * p)), spec((p, 8)), spec((p, 8)))
    return pl.pallas_call(
        functools.partial(_prep_kernel, chunk=chunk, width=width, n_pow=n_pow, short=short),
        grid=(g // gb,), in_specs=in_specs, out_specs=out_specs, out_shape=out_shape,
        compiler_params=_cparams(("arbitrary",)), name=f"s5_prep_{chunk}")(*args)


def _phase_copies(x5_hbm, buf, sem, step, slot, *, per_step, to_hbm=False):
    nseq, nq, _, chunk, _ = x5_hbm.shape
    copies = []
    for ph in range(per_step):
        gp = step * per_step + ph
        r, s = gp // chunk, gp % chunk
        for seq in range(nseq):
            hbm = x5_hbm.at[seq, :, r, s, :]
            vmem = buf.at[slot, ph, pl.ds(seq * nq, nq), :]
            copies.append(pltpu.make_async_copy(vmem, hbm, sem.at[slot]) if to_hbm
                          else pltpu.make_async_copy(hbm, vmem, sem.at[slot]))
    return copies


def _grid_step():
    return (pl.program_id(0) * pl.num_programs(1) + pl.program_id(1),
            pl.num_programs(0) * pl.num_programs(1))


def _fetch_phases(x5_hbm, buf, sem, *, per_step):
    step, n_steps = _grid_step()
    slot = step % 2
    copies = functools.partial(_phase_copies, x5_hbm, buf, sem, per_step=per_step)

    @pl.when(step == 0)
    def _():
        for c in copies(0, 0):
            c.start()

    @pl.when(step + 1 < n_steps)
    def _():
        for c in copies(step + 1, 1 - slot):
            c.start()

    for c in copies(step, slot):
        c.wait()
    return slot


def _store_phases(o5_hbm, buf, sem, fill, *, per_step):
    step, n_steps = _grid_step()
    slot = step % 2
    copies = functools.partial(_phase_copies, o5_hbm, buf, sem, per_step=per_step, to_hbm=True)

    @pl.when(step >= 2)
    def _():
        for c in copies(step - 2, slot):
            c.wait()

    fill(slot)
    for c in copies(step, slot):
        c.start()

    @pl.when(step == n_steps - 1)
    def _():
        @pl.when(step >= 1)
        def _():
            for c in copies(step - 1, 1 - slot):
                c.wait()
        for c in copies(step, slot):
            c.wait()


def _inproj0_kernel(x5_hbm, g_ref, w_ref, u_ref, gate_ref, xbuf, sem):
    slot = _fetch_phases(x5_hbm, xbuf, sem, per_step=1)
    xn = _rms_norm(xbuf[slot, 0], g_ref[...]).astype(BF16)
    res = lax.dot_general(w_ref[...], xn, (((1,), (1,)), ((), ())),
                          preferred_element_type=F32)
    u_ref[...] = res[:D_INNER].reshape(SSM_GROUPS, SSM_GROUP, -1).astype(BF16)
    gate_ref[...] = res[D_INNER:].astype(BF16)


def _inproj0(x5, g, w_t):
    nseq, nq, n_tiles, chunk, _ = x5.shape
    tc = nseq * nq
    nc = tc * n_tiles
    return pl.pallas_call(
        _inproj0_kernel,
        grid=(n_tiles, chunk),
        in_specs=[pl.BlockSpec(memory_space=pl.ANY),
                  pl.BlockSpec((1, D_MODEL), lambda i, s: (0, 0)),
                  pl.BlockSpec((2 * D_INNER, D_MODEL), lambda i, s: (0, 0))],
        out_specs=(pl.BlockSpec((SSM_GROUPS, SSM_GROUP, tc), lambda i, s: (0, s, i)),
                   pl.BlockSpec((None, D_INNER, tc), lambda i, s: (s, 0, i))),
        out_shape=(jax.ShapeDtypeStruct((SSM_GROUPS, SSM_GROUP * chunk, nc), BF16),
                   jax.ShapeDtypeStruct((chunk, D_INNER, nc), BF16)),
        scratch_shapes=[pltpu.VMEM((2, 1, tc, D_MODEL), F32), pltpu.SemaphoreType.DMA((2,))],
        compiler_params=_cparams(("arbitrary", "arbitrary")), name=f"l0_inproj_{chunk}")(x5, g, w_t)


def _ssm_kernel(*refs, chunk, n_tiles, segs, carried):
    if carried:
        (u_ref, m_ref, w_ref, v_ref, dre_ref, dim_ref, d_ref, h0re_ref, h0im_ref,
         y_ref, hre_ref, him_ref, sre_ref, sim_ref) = refs
    else:
        (u_ref, m_ref, w_ref, v_ref, dre_ref, dim_ref, d_ref, y_ref, hfin_ref,
         sre_ref, sim_ref, ere_ref, eim_ref, tt_ref) = refs
    nc = u_ref.shape[-1]
    tl = nc // n_tiles
    p = SSM_STATE
    gb = u_ref.shape[0]
    groups = range(gb)

    def tile(r):
        return slice(r * tl, (r + 1) * tl)

    def cmul(ar, ai, br, bi):
        return ar * br - ai * bi, ar * bi + ai * br

    for j in groups:
        z = jnp.dot(w_ref[j], u_ref[j], preferred_element_type=F32)
        sre_ref[j] = z[:p]
        sim_ref[j] = z[p:]
    if carried:
        h0re = h0re_ref[...].T
        h0im = h0im_ref[...].T
        for j in groups:
            dr, di = cmul(dre_ref[j][:, 0:1], dim_ref[j][:, 0:1],
                          h0re[p * j:p * (j + 1)], h0im[p * j:p * (j + 1)])
            sre_ref[j] = sre_ref[j] + dr
            sim_ref[j] = sim_ref[j] + di
        hre_ref[...] = sre_ref[...].reshape(gb * p, nc).T
        him_ref[...] = sim_ref[...].reshape(gb * p, nc).T
    else:
        lane = lax.broadcasted_iota(jnp.int32, (1, tl), 1)
        posq = lane & (segs - 1)
        for r in range(1, n_tiles):
            for j in groups:
                dr, di = cmul(dre_ref[j][:, 0:1], dim_ref[j][:, 0:1],
                              sre_ref[j, :, tile(r - 1)], sim_ref[j, :, tile(r - 1)])
                sre_ref[j, :, tile(r)] = sre_ref[j, :, tile(r)] + dr
                sim_ref[j, :, tile(r)] = sim_ref[j, :, tile(r)] + di
        for j in groups:
            ere_ref[j] = sre_ref[j, :, tile(n_tiles - 1)]
            eim_ref[j] = sim_ref[j, :, tile(n_tiles - 1)]
        col0 = n_tiles.bit_length() - 1
        for k in range(segs.bit_length() - 1):
            keep = posq >= (1 << k)
            for j in groups:
                ar = jnp.where(keep, dre_ref[j][:, col0 + k:col0 + k + 1], 0.0)
                ai = jnp.where(keep, dim_ref[j][:, col0 + k:col0 + k + 1], 0.0)
                t_re, t_im = ere_ref[j], eim_ref[j]
                dr, di = cmul(ar, ai, pltpu.roll(t_re, 1 << k, 1), pltpu.roll(t_im, 1 << k, 1))
                ere_ref[j] = t_re + dr
                eim_ref[j] = t_im + di
        first = posq == 0
        nseq = tl // segs
        for j in groups:
            tt_ref[j] = jnp.concatenate([ere_ref[j], eim_ref[j]], axis=0).T
            hfin_ref[j] = tt_ref[j, pl.ds(segs - 1, nseq, stride=segs), :]
            ere_ref[j] = jnp.where(first, 0.0, pltpu.roll(ere_ref[j], 1, 1))
            eim_ref[j] = jnp.where(first, 0.0, pltpu.roll(eim_ref[j], 1, 1))
        for j in groups:
            a_re, a_im = dre_ref[j][:, 0:1], dim_ref[j][:, 0:1]
            pw_re, pw_im = a_re, a_im
            e_re, e_im = ere_ref[j], eim_ref[j]
            for r in range(n_tiles):
                dr, di = cmul(pw_re, pw_im, e_re, e_im)
                sre_ref[j, :, tile(r)] = sre_ref[j, :, tile(r)] + dr
                sim_ref[j, :, tile(r)] = sim_ref[j, :, tile(r)] + di
                pw_re, pw_im = cmul(pw_re, pw_im, a_re, a_im)
    for j in groups:
        u = u_ref[j]
        if carried:
            pre_re, pre_im = h0re[p * j:p * (j + 1)], h0im[p * j:p * (j + 1)]
        elif n_tiles == 1:
            pre_re, pre_im = ere_ref[j], eim_ref[j]
        else:
            pre_re = jnp.concatenate([ere_ref[j], sre_ref[j, :, :nc - tl]], axis=1)
            pre_im = jnp.concatenate([eim_ref[j], sim_ref[j, :, :nc - tl]], axis=1)
        prev = jnp.concatenate([pre_re, pre_im], axis=0).astype(BF16)
        y = (jnp.dot(m_ref[j], u, preferred_element_type=F32)
             + jnp.dot(v_ref[j], prev, preferred_element_type=F32)
             + d_ref[j] * u.astype(F32))
        y_ref[:, SSM_GROUP * j:SSM_GROUP * (j + 1), :] = (
            y.reshape(chunk, SSM_GROUP, nc).astype(BF16))


def _ssm(u2, m, w, v, dre, dim, d_col, h0, chunk, n_tiles, segs):
    g, kl, nc = u2.shape
    carried = h0 is not None
    gb, p = (CARRIED_GROUP_BLOCK if carried else GROUP_BLOCK), SSM_STATE
    tl = nc // n_tiles
    wspec = lambda shape: pl.BlockSpec((gb,) + shape, lambda a: (a, 0, 0))
    in_specs = [wspec((kl, nc)), wspec((kl, kl)), wspec((2 * p, kl)), wspec((kl, 2 * p)),
                wspec((p, 8)), wspec((p, 8)), wspec((kl, 1))]
    args = [u2, m, w, v, dre, dim, d_col]
    scratch = [pltpu.VMEM((gb, p, nc), F32), pltpu.VMEM((gb, p, nc), F32)]
    y_spec = pl.BlockSpec((chunk, SSM_GROUP * gb, nc), lambda a: (0, a, 0))
    y_shape = jax.ShapeDtypeStruct((chunk, D_INNER, nc), BF16)
    if carried:
        assert n_tiles == 1
        state_spec = pl.BlockSpec((nc, gb * p), lambda a: (0, a))
        in_specs += [state_spec] * 2
        args += list(h0)
        out_specs = (y_spec, state_spec, state_spec)
        out_shape = (y_shape,) + (jax.ShapeDtypeStruct((nc, g * p), F32),) * 2
    else:
        nseq = tl // segs
        scratch += [pltpu.VMEM((gb, p, tl), F32), pltpu.VMEM((gb, p, tl), F32),
                    pltpu.VMEM((gb, tl, 2 * p), F32)]
        out_specs = (y_spec, wspec((nseq, 2 * p)))
        out_shape = (y_shape, jax.ShapeDtypeStruct((g, nseq, 2 * p), F32))
    return pl.pallas_call(
        functools.partial(_ssm_kernel, chunk=chunk, n_tiles=n_tiles, segs=segs, carried=carried),
        grid=(g // gb,), in_specs=in_specs, out_specs=out_specs, out_shape=out_shape,
        scratch_shapes=scratch,
        compiler_params=_cparams(("arbitrary",)), name=f"s5_scan_{chunk}")(*args)


POST_PHASES = 2


def _post0_kernel(y_ref, gate_ref, x5_hbm, wglu_ref, bglu_ref, wout_ref, o_ref, xbuf, sem, *scratch,
                  phase_major_out):
    slot = _fetch_phases(x5_hbm, xbuf, sem, per_step=POST_PHASES)
    phases = range(POST_PHASES)
    ys = [y_ref[ph].astype(F32) for ph in phases]
    ys = [0.5 * y * (1.0 + lax.erf(y * math.sqrt(0.5))) for y in ys]
    zs = [jnp.dot(wglu_ref[...], y.astype(BF16), preferred_element_type=F32) + bglu_ref[...] for y in ys]
    ys = [y * _sigmoid(z) * _silu(gate_ref[ph].astype(F32)) for ph, y, z in zip(phases, ys, zs)]
    outs = [jnp.dot(wout_ref[...], y.astype(BF16), preferred_element_type=F32) for y in ys]
    if phase_major_out:
        for ph, o in zip(phases, outs):
            o_ref[ph] = xbuf[slot, ph] + o.T
    else:
        obuf, osem = scratch

        def fill(oslot):
            for ph, o in zip(phases, outs):
                obuf[oslot, ph] = xbuf[slot, ph] + o.T

        _store_phases(o_ref, obuf, osem, fill, per_step=POST_PHASES)


def _post0(y2, gate2, x5, wglu_t, bglu_col, wout_t, phase_major_out):
    nseq, nq, n_tiles, chunk, _ = x5.shape
    tc = nseq * nq
    nc = tc * n_tiles
    ph = POST_PHASES
    scratch = [pltpu.VMEM((2, ph, tc, D_MODEL), F32), pltpu.SemaphoreType.DMA((2,))]
    if phase_major_out:
        out_shape = jax.ShapeDtypeStruct((chunk, nc, D_MODEL), F32)
        out_spec = pl.BlockSpec((ph, tc, D_MODEL), lambda r, s: (s, r, 0))
    else:
        out_shape = jax.ShapeDtypeStruct(x5.shape, F32)
        out_spec = pl.BlockSpec(memory_space=pl.ANY)
        scratch = scratch * 2
    return pl.pallas_call(
        functools.partial(_post0_kernel, phase_major_out=phase_major_out),
        grid=(n_tiles, chunk // ph),
        in_specs=[pl.BlockSpec((ph, D_INNER, tc), lambda r, s: (s, 0, r)),
                  pl.BlockSpec((ph, D_INNER, tc), lambda r, s: (s, 0, r)),
                  pl.BlockSpec(memory_space=pl.ANY),
                  pl.BlockSpec((D_INNER, D_INNER), lambda r, s: (0, 0)),
                  pl.BlockSpec((D_INNER, 1), lambda r, s: (0, 0)),
                  pl.BlockSpec((D_MODEL, D_INNER), lambda r, s: (0, 0))],
        out_specs=out_spec, out_shape=out_shape,
        scratch_shapes=scratch,
        compiler_params=_cparams(("arbitrary", "arbitrary")), name=f"l0_post_{chunk}")(
            y2, gate2, x5, wglu_t, bglu_col, wout_t)


def _layer_norm_act(z, gate, lng, lnb):
    mu = jnp.mean(z, axis=-1, keepdims=True)
    zc = z - mu
    var = jnp.mean(zc * zc, axis=-1, keepdims=True)
    zn = zc * lax.rsqrt(var + EPS) * lng + lnb
    return _silu(zn) * _silu(gate)


CONV_COLS = 256
HIST_CHUNKS = 8


def _conv_taps(lp):
    taps = {}
    for t in range(lp):
        for k in range(CONV_WIDTH):
            o = t + k - CONV_HIST
            delta = (-o + lp - 1) // lp if o < 0 else 0
            taps[t, k] = (o + lp * delta, delta)
    return taps


def _conv_prompt_kernel(x4_hbm, g_ref, w_ref, dww_ref, dwb_ref, lng_ref, lnb_ref, wout_ref, gf_ref,
                        y4_hbm, cst_ref, xn_ref, vbuf_ref, vsh_ref, gate_ref, zc_ref,
                        xbuf, xsem, ybuf, ysem, *, lp, ct, tiles, n_seq):
    j = pl.program_id(1)
    hist = HIST_CHUNKS
    taps = _conv_taps(lp)
    shifted = sorted({sd for sd in taps.values() if sd[1] > 0})
    shift_slot = {sd: i for i, sd in enumerate(shifted)}
    n_blocks = D_INNER // CONV_COLS
    step = pl.program_id(0) * tiles + j
    n_steps = n_seq * tiles
    buf = step % 2

    def tile_copies(hbm, vmem, sem, stp, slt, to_hbm):
        seq, tile = stp // tiles, stp % tiles
        pairs = [(hbm.at[seq, pl.ds(tile * ct, ct), t, :], vmem.at[slt, t]) for t in range(lp)]
        return [pltpu.make_async_copy(v, h, sem.at[slt]) if to_hbm else
                pltpu.make_async_copy(h, v, sem.at[slt]) for h, v in pairs]

    x_copies = functools.partial(tile_copies, x4_hbm, xbuf, xsem, to_hbm=False)
    y_copies = functools.partial(tile_copies, y4_hbm, ybuf, ysem, to_hbm=True)

    @pl.when(step == 0)
    def _():
        for c in x_copies(0, 0):
            c.start()

    @pl.when(step + 1 < n_steps)
    def _():
        for c in x_copies(step + 1, 1 - buf):
            c.start()

    for c in x_copies(step, buf):
        c.wait()

    @pl.when(j == 0)
    def _():
        vbuf_ref[:, 0:hist, :] = jnp.zeros((lp, hist, D_INNER), F32)

    g = g_ref[...]
    for s in range(lp):
        xn_ref[ct * s:ct * (s + 1), :] = _rms_norm(xbuf[buf, s], g).astype(BF16)

    def lanes(start):
        return pl.ds(pl.multiple_of(start, CONV_COLS), CONV_COLS)

    def project(c):
        xn = xn_ref[...]
        a = jnp.dot(xn, w_ref[:, lanes(c * CONV_COLS)], preferred_element_type=F32)
        b = jnp.dot(xn, w_ref[:, lanes(D_INNER + c * CONV_COLS)], preferred_element_type=F32)
        cols = lanes(c * CONV_COLS)
        gate_ref[:, cols] = jnp.dot(xn, w_ref[:, lanes(2 * D_INNER + c * CONV_COLS)],
                                    preferred_element_type=F32)
        v = a * _sigmoid(b)
        for s in range(lp):
            vbuf_ref[s, hist:hist + ct, cols] = v[ct * s:ct * (s + 1)]
        for (s, delta), i in shift_slot.items():
            vsh_ref[i, :, cols] = vbuf_ref[s, hist - delta:hist - delta + ct, cols]

    def conv(c):
        cols = lanes(c * CONV_COLS)
        for t in range(lp):
            acc = jnp.broadcast_to(dwb_ref[:, cols], (ct, CONV_COLS))
            for k in range(CONV_WIDTH):
                s, delta = taps[t, k]
                slab = vsh_ref[shift_slot[s, delta], :, cols] if delta else vbuf_ref[s, hist:hist + ct, cols]
                acc = acc + dww_ref[k:k + 1, cols] * slab
            zc_ref[ct * t:ct * (t + 1), cols] = acc

    project(0)

    def block(c, carry):
        conv(c)
        project(c + 1)
        return carry

    lax.fori_loop(0, n_blocks - 1, block, 0)
    conv(n_blocks - 1)
    zact = _layer_norm_act(zc_ref[...], gate_ref[...], lng_ref[...], lnb_ref[...]).astype(BF16)
    o = jnp.dot(zact, wout_ref[...], preferred_element_type=F32)
    gf = gf_ref[...]

    @pl.when(step >= 2)
    def _():
        for c in y_copies(step - 2, buf):
            c.wait()

    for t in range(lp):
        ybuf[buf, t] = _rms_norm(xbuf[buf, t] + o[ct * t:ct * (t + 1)], gf)
    for c in y_copies(step, buf):
        c.start()

    @pl.when(step == n_steps - 1)
    def _():
        if n_steps >= 2:
            for c in y_copies(step - 1, 1 - buf):
                c.wait()
        for c in y_copies(step, buf):
            c.wait()

    @pl.when(j == tiles - 1)
    def _():
        for i in range(CONV_HIST):
            tok = lp * ct - CONV_HIST + i
            cst_ref[i:i + 1, :] = vbuf_ref[tok % lp, hist + tok // lp:hist + tok // lp + 1, :]

    vbuf_ref[:, 0:hist, :] = vbuf_ref[:, ct:ct + hist, :]


def _layer1_prompt(x4, g, w_in, dw_w, dw_b, ln_g, ln_b, w_out, gf, ct):
    n, cps, lp, _ = x4.shape
    tiles = cps // ct
    n_shift = len({sd for sd in _conv_taps(lp).values() if sd[1] > 0})
    const = lambda shape: pl.BlockSpec(shape, lambda a, j: (0,) * len(shape), pipeline_mode=pl.Buffered(1))
    return pl.pallas_call(
        functools.partial(_conv_prompt_kernel, lp=lp, ct=ct, tiles=tiles, n_seq=n),
        grid=(n, tiles),
        in_specs=[pl.BlockSpec(memory_space=pl.ANY),
                  const((1, D_MODEL)), const((D_MODEL, 3 * D_INNER)),
                  const((CONV_WIDTH, D_INNER)), const((1, D_INNER)),
                  const((1, D_INNER)), const((1, D_INNER)),
                  const((D_INNER, D_MODEL)), const((1, D_MODEL))],
        out_specs=(pl.BlockSpec(memory_space=pl.ANY),
                   pl.BlockSpec((None, CONV_HIST, D_INNER), lambda a, j: (a, 0, 0))),
        out_shape=(jax.ShapeDtypeStruct(x4.shape, F32),
                   jax.ShapeDtypeStruct((n, CONV_HIST, D_INNER), F32)),
        scratch_shapes=[pltpu.VMEM((lp * ct, D_MODEL), BF16),
                        pltpu.VMEM((lp, HIST_CHUNKS + ct, D_INNER), F32),
                        pltpu.VMEM((n_shift, ct, D_INNER), F32),
                        pltpu.VMEM((lp * ct, D_INNER), F32),
                        pltpu.VMEM((lp * ct, D_INNER), F32),
                        pltpu.VMEM((2, lp, ct, D_MODEL), F32), pltpu.SemaphoreType.DMA((2,)),
                        pltpu.VMEM((2, lp, ct, D_MODEL), F32), pltpu.SemaphoreType.DMA((2,))],
        compiler_params=_cparams(("arbitrary", "arbitrary")), name="l1_prompt")(
            x4, g, w_in, dw_w, dw_b, ln_g, ln_b, w_out, gf)


def _inproj1_kernel(x_ref, g_ref, w_ref, v_ref, gate_ref):
    xn = _rms_norm(x_ref[...], g_ref[...]).astype(BF16)
    abg = jnp.dot(xn, w_ref[...], preferred_element_type=F32)
    v_ref[...] = abg[:, :D_INNER] * _sigmoid(abg[:, D_INNER:2 * D_INNER])
    gate_ref[...] = abg[:, 2 * D_INNER:]


def _inproj1(x1, g, w_in, tm):
    r = x1.shape[0]
    const = lambda shape: pl.BlockSpec(shape, lambda i: (0,) * len(shape))
    return pl.pallas_call(
        _inproj1_kernel, grid=(r // tm,),
        in_specs=[pl.BlockSpec((tm, D_MODEL), lambda i: (i, 0)), const((1, D_MODEL)),
                  const((D_MODEL, 3 * D_INNER))],
        out_specs=(pl.BlockSpec((tm, D_INNER), lambda i: (i, 0)),) * 2,
        out_shape=(jax.ShapeDtypeStruct((r, D_INNER), F32),) * 2,
        compiler_params=_cparams(("arbitrary",)), name="l1_inproj_sample")(x1, g, w_in)


SAMPLE_SEQ_TILE = 16
SAMPLE_COLS = 1024


def _conv_sample_kernel(cache_ref, v_ref, gate_ref, dww_ref, dwb_ref, lng_ref, lnb_ref,
                        z_ref, cst_ref, acc_ref, *, steps):
    ns = cache_ref.shape[0]
    for t in range(steps):
        for c0 in range(0, D_INNER, SAMPLE_COLS):
            acc = jnp.broadcast_to(dwb_ref[:, c0:c0 + SAMPLE_COLS], (ns, SAMPLE_COLS))
            for k in range(CONV_WIDTH):
                jrow = t + k
                if jrow < CONV_HIST:
                    src = cache_ref[:, jrow * D_INNER + c0:jrow * D_INNER + c0 + SAMPLE_COLS]
                else:
                    src = v_ref[jrow - CONV_HIST, :, c0:c0 + SAMPLE_COLS]
                acc = acc + dww_ref[k:k + 1, c0:c0 + SAMPLE_COLS] * src
            acc_ref[t, :, c0:c0 + SAMPLE_COLS] = acc
        z_ref[t] = _layer_norm_act(acc_ref[t], gate_ref[t], lng_ref[...], lnb_ref[...]).astype(BF16)
    keep = CONV_HIST - steps
    cst_ref[:, 0:keep * D_INNER] = cache_ref[:, steps * D_INNER:CONV_HIST * D_INNER]
    for t in range(steps):
        cst_ref[:, (keep + t) * D_INNER:(keep + t + 1) * D_INNER] = v_ref[t]


def _conv_sample(cache2d, v3, gate3, dw_w, dw_b, ln_g, ln_b):
    steps, n, _ = v3.shape
    ns = SAMPLE_SEQ_TILE
    const = lambda shape: pl.BlockSpec(shape, lambda i: (0,) * len(shape))
    return pl.pallas_call(
        functools.partial(_conv_sample_kernel, steps=steps), grid=(n // ns,),
        in_specs=[pl.BlockSpec((ns, CONV_HIST * D_INNER), lambda i: (i, 0)),
                  pl.BlockSpec((steps, ns, D_INNER), lambda i: (0, i, 0)),
                  pl.BlockSpec((steps, ns, D_INNER), lambda i: (0, i, 0)),
                  const((CONV_WIDTH, D_INNER)), const((1, D_INNER)), const((1, D_INNER)),
                  const((1, D_INNER))],
        out_specs=(pl.BlockSpec((steps, ns, D_INNER), lambda i: (0, i, 0)),
                   pl.BlockSpec((ns, CONV_HIST * D_INNER), lambda i: (i, 0))),
        out_shape=(jax.ShapeDtypeStruct((steps, n, D_INNER), BF16),
                   jax.ShapeDtypeStruct((n, CONV_HIST * D_INNER), F32)),
        scratch_shapes=[pltpu.VMEM((steps, ns, D_INNER), F32)],
        compiler_params=_cparams(("arbitrary",)), name="l1_conv_sample")(
            cache2d, v3, gate3, dw_w, dw_b, ln_g, ln_b)


def _out1_kernel(z_ref, x_ref, w_ref, g_ref, y_ref):
    x2 = x_ref[...] + jnp.dot(z_ref[...], w_ref[...], preferred_element_type=F32)
    y_ref[...] = _rms_norm(x2, g_ref[...])


def _out1(z, x1, w_out, g, tm):
    r = x1.shape[0]
    phases = x1.shape[1] // D_MODEL
    const = lambda shape: pl.BlockSpec(shape, lambda i, s: (0,) * len(shape))
    return pl.pallas_call(
        _out1_kernel, grid=(r // tm, phases),
        in_specs=[pl.BlockSpec((tm, D_INNER), lambda i, s: (i, s)),
                  pl.BlockSpec((tm, D_MODEL), lambda i, s: (i, s)),
                  const((D_INNER, D_MODEL)), const((1, D_MODEL))],
        out_specs=pl.BlockSpec((tm, D_MODEL), lambda i, s: (i, s)),
        out_shape=jax.ShapeDtypeStruct((r, phases * D_MODEL), F32),
        compiler_params=_cparams(("arbitrary", "arbitrary")), name="l1_out")(z, x1, w_out, g)


WEIGHT_BLOCK = 1024


def _to_bf16_kernel(w_ref, o_ref, *, transpose):
    w = w_ref[...]
    o_ref[...] = (w.T if transpose else w).astype(BF16)


def _to_bf16(w, transpose):
    r, c = w.shape
    b = WEIGHT_BLOCK
    out_spec = pl.BlockSpec((b, b), (lambda i, j: (j, i)) if transpose else (lambda i, j: (i, j)))
    return pl.pallas_call(
        functools.partial(_to_bf16_kernel, transpose=transpose), grid=(r // b, c // b),
        in_specs=[pl.BlockSpec((b, b), lambda i, j: (i, j))], out_specs=out_spec,
        out_shape=jax.ShapeDtypeStruct((c, r) if transpose else (r, c), BF16),
        compiler_params=_cparams(("arbitrary", "arbitrary")), name="weight_bf16")(w)


def kernel(x_prompt, x_sample, state_ssm_re, state_ssm_im, cache_conv, norm_g, final_norm_g, ssm_w_in, ssm_a_re, ssm_a_im, ssm_log_dt, ssm_b_re, ssm_b_im, ssm_c_re, ssm_c_im, ssm_d, ssm_w_glu, ssm_b_glu, ssm_w_out, conv_w_in, conv_dw_w, conv_dw_b, conv_ln_g, conv_ln_b, conv_w_out):
    n_p, t_p, _ = x_prompt.shape
    n_s, t_s, _ = x_sample.shape
    g, p = SSM_GROUPS, SSM_STATE

    w_in0_t = _to_bf16(ssm_w_in[0], transpose=True)
    w_glu_t = _to_bf16(ssm_w_glu[0], transpose=True)
    b_glu_col = ssm_b_glu[0].reshape(D_INNER, 1)
    w_out0_t = _to_bf16(ssm_w_out[0], transpose=True)
    w_in1 = _to_bf16(conv_w_in[0], transpose=False)
    w_out1 = _to_bf16(conv_w_out[0], transpose=False)
    g0 = norm_g[0].reshape(1, D_MODEL)
    g1 = norm_g[1].reshape(1, D_MODEL)
    gf = final_norm_g.reshape(1, D_MODEL)
    dw_w, dw_b = conv_dw_w[0], conv_dw_b[0].reshape(1, D_INNER)
    ln_g, ln_b = conv_ln_g[0].reshape(1, D_INNER), conv_ln_b[0].reshape(1, D_INNER)

    def layer0(x5, ops, h0, phase_major_out):
        _, segs, n_tiles, chunk, _ = x5.shape
        m, w, v, dre, dim = ops
        d_col = jnp.tile(ssm_d[0].reshape(g, 1, SSM_GROUP), (1, chunk, 1)).reshape(g, chunk * SSM_GROUP, 1)
        u2, gate2 = _inproj0(x5, g0, w_in0_t)
        y2, *states = _ssm(u2, m, w, v, dre, dim, d_col, h0, chunk, n_tiles, segs)
        x1 = _post0(y2, gate2, x5, w_glu_t, b_glu_col, w_out0_t, phase_major_out)
        return x1, states

    lp = PROMPT_CHUNK
    cps = t_p // lp
    segs = cps // SCAN_TILES
    assert t_s <= lp and lp % t_s == 0
    ops_p = _ssm_prep(ssm_a_re[0], ssm_a_im[0], ssm_log_dt[0], ssm_b_re[0], ssm_b_im[0],
                      ssm_c_re[0], ssm_c_im[0], lp, cps.bit_length() - 1, t_s)
    x1p, (hfin,) = layer0(x_prompt.reshape(n_p, segs, SCAN_TILES, lp, D_MODEL), ops_p, None, False)
    hfin = jnp.transpose(hfin, (1, 0, 2))
    ssm_re_p, ssm_im_p = hfin[None, :, :, :p], hfin[None, :, :, p:]
    y4, conv_p = _layer1_prompt(x1p.reshape(n_p, cps, lp, D_MODEL), g1, w_in1, dw_w, dw_b, ln_g, ln_b,
                                w_out1, gf, L1_CHUNK_ROWS)
    y_prompt = y4.reshape(n_p, t_p, D_MODEL)

    h0 = (state_ssm_re[0].reshape(n_s, g * p), state_ssm_im[0].reshape(n_s, g * p))
    m_p, w_p, v_p, dre_p, dim_p = ops_p
    kl_s = SSM_GROUP * t_s
    short_decay = lambda d: jnp.broadcast_to(d[:, :, 7:8], d.shape)
    ops_s = (m_p[:, :kl_s, :kl_s], w_p[:, :, SSM_GROUP * lp - kl_s:], v_p[:, :kl_s, :],
             short_decay(dre_p), short_decay(dim_p))
    x1s, (hre_s, him_s) = layer0(x_sample.reshape(1, n_s, 1, t_s, D_MODEL), ops_s, h0, True)
    ssm_re_s = hre_s.reshape(1, n_s, g, p)
    ssm_im_s = him_s.reshape(1, n_s, g, p)
    x1s = x1s.reshape(t_s * n_s, D_MODEL)
    v_s, gate_s = _inproj1(x1s, g1, w_in1, tm=256)
    zs, conv_s = _conv_sample(cache_conv[0].reshape(n_s, CONV_HIST * D_INNER),
                              v_s.reshape(t_s, n_s, D_INNER), gate_s.reshape(t_s, n_s, D_INNER),
                              dw_w, dw_b, ln_g, ln_b)
    y_s = _out1(zs.reshape(t_s * n_s, D_INNER), x1s, w_out1, gf, tm=t_s * n_s)
    y_sample = jnp.transpose(y_s.reshape(t_s, n_s, D_MODEL), (1, 0, 2))

    return (y_prompt, y_sample, ssm_re_p, ssm_im_p, conv_p[None],
            ssm_re_s, ssm_im_s, conv_s.reshape(1, n_s, CONV_HIST, D_INNER))
```

```python
import functools
import math

import jax
import jax.numpy as jnp
from jax import lax
from jax.experimental import pallas as pl
from jax.experimental.pallas import tpu as pltpu

D_MODEL = 1024
D_INNER = 2048
SSM_GROUP = 16
SSM_GROUPS = D_INNER // SSM_GROUP
SSM_STATE = 64
CONV_WIDTH = 31
CONV_HIST = CONV_WIDTH - 1
EPS = 1e-6

F32 = jnp.float32
BF16 = jnp.bfloat16

LANES = 128
PROMPT_CHUNK = 16
GROUP_BLOCK = 8
CARRIED_GROUP_BLOCK = 32
SCAN_TILES = 4
L1_CHUNK_ROWS = 32
VMEM_LIMIT = 56 * 1024 * 1024


def _cparams(sem):
    return pltpu.CompilerParams(dimension_semantics=sem, vmem_limit_bytes=VMEM_LIMIT)


def _rms_norm(x, g):
    ms = jnp.mean(x * x, axis=-1, keepdims=True)
    return x * lax.rsqrt(ms + EPS) * g


def _sigmoid(x):
    return 1.0 / (1.0 + jnp.exp(-x))


def _silu(x):
    return x * _sigmoid(x)


def _cpow(lre, lim, k, nbits, shape):
    pr = jnp.ones(shape, F32)
    pi = jnp.zeros(shape, F32)
    sr, si = lre, lim
    for b in range(nbits):
        take = (lax.shift_right_logical(k, b) & 1) == 1
        pr, pi = (jnp.where(take, pr * sr - pi * si, pr), jnp.where(take, pr * si + pi * sr, pi))
        if b + 1 < nbits:
            sr, si = sr * sr - si * si, 2.0 * sr * si
    return pr, pi


def _prep_kernel(are_ref, aim_ref, ldt_ref, bre_t, bim_t, cre, cim, cre_tt, cim_tt,
                 m_ref, w_ref, v_ref, dre_ref, dim_ref, *, chunk, width, n_pow, short):
    kl = SSM_GROUP * chunk
    reps = width // SSM_GROUP
    lane = lax.broadcasted_iota(jnp.int32, (1, width), 1)
    step = lax.shift_right_logical(lane, 4)
    kexp = jnp.maximum(chunk - 1 - step, 0)
    kv = step + 1
    lane8 = lax.broadcasted_iota(jnp.int32, (1, 8), 1)
    dt_all = jnp.exp(ldt_ref[...])
    ar_all = are_ref[...]
    ai_all = aim_ref[...]
    mag = jnp.exp(ar_all * dt_all)
    lre_all = mag * jnp.cos(ai_all * dt_all)
    lim_all = mag * jnp.sin(ai_all * dt_all)
    den = ar_all * ar_all + ai_all * ai_all
    fre_all = ((lre_all - 1.0) * ar_all + lim_all * ai_all) / den
    fim_all = (lim_all * ar_all - (lre_all - 1.0) * ai_all) / den
    for j in range(GROUP_BLOCK):
        lre, lim = lre_all[:, j:j + 1], lim_all[:, j:j + 1]
        fre, fim = fre_all[:, j:j + 1], fim_all[:, j:j + 1]
        bre = jnp.tile(bre_t[j], (1, reps))
        bim = jnp.tile(bim_t[j], (1, reps))
        bbre = fre * bre - fim * bim
        bbim = fre * bim + fim * bre
        pre, pim = _cpow(lre, lim, kexp, (chunk - 1).bit_length(), (SSM_STATE, width))
        wre = pre * bbre - pim * bbim
        wim = pre * bbim + pim * bbre
        w_ref[j, 0:SSM_STATE, :] = wre[:, :kl].astype(BF16)
        w_ref[j, SSM_STATE:2 * SSM_STATE, :] = wim[:, :kl].astype(BF16)
        ktab = (jnp.dot(cre[j], wre, precision=lax.Precision.HIGHEST, preferred_element_type=F32)
                - jnp.dot(cim[j], wim, precision=lax.Precision.HIGHEST, preferred_element_type=F32))
        for t in range(chunk):
            shift = (width - SSM_GROUP * (chunk - 1 - t)) % width
            r = pltpu.roll(ktab, shift, 1) if shift else ktab
            blk = jnp.where(lane < SSM_GROUP * (t + 1), r, 0.0)
            m_ref[j, SSM_GROUP * t:SSM_GROUP * (t + 1), :] = blk[:, :kl].astype(BF16)
        qre, qim = _cpow(lre, lim, kv, (width // SSM_GROUP).bit_length(), (SSM_STATE, width))
        ct_re = jnp.tile(cre_tt[j], (1, reps))
        ct_im = jnp.tile(cim_tt[j], (1, reps))
        v_t = jnp.concatenate([ct_re * qre - ct_im * qim, -(ct_re * qim + ct_im * qre)], axis=0)
        v_ref[j] = v_t.T[:kl, :].astype(BF16)
        sr, si = lre, lim
        dre = jnp.zeros((SSM_STATE, 8), F32)
        dim = jnp.zeros((SSM_STATE, 8), F32)
        for e in range(chunk.bit_length() - 1):
            if (1 << e) == short:
                dre = jnp.where(lane8 == 7, sr, dre)
                dim = jnp.where(lane8 == 7, si, dim)
            sr, si = sr * sr - si * si, 2.0 * sr * si
        for k in range(n_pow):
            dre = jnp.where(lane8 == k, sr, dre)
            dim = jnp.where(lane8 == k, si, dim)
            if k + 1 < n_pow:
                sr, si = sr * sr - si * si, 2.0 * sr * si
        dre_ref[j] = dre
        dim_ref[j] = dim


def _ssm_prep(a_re, a_im, log_dt, b_re, b_im, c_re, c_im, chunk, n_pow, short):
    g, p, h = SSM_GROUPS, SSM_STATE, SSM_GROUP
    kl = h * chunk
    width = max(kl, LANES)
    gb = GROUP_BLOCK
    cols = lambda a: jnp.transpose(a.reshape(g // gb, gb, p), (0, 2, 1))
    args = (cols(a_re), cols(a_im), log_dt.reshape(g // gb, 1, gb), b_re, b_im,
            c_re, c_im, jnp.swapaxes(c_re, 1, 2), jnp.swapaxes(c_im, 1, 2))

    def spec(shape):
        return pl.BlockSpec((gb,) + shape, lambda i: (i, 0, 0))

    def blockwise(shape):
        return pl.BlockSpec((None,) + shape, lambda i: (i, 0, 0))

    in_specs = [blockwise((p, gb)), blockwise((p, gb)), blockwise((1, gb)),
                spec((p, h)), spec((p, h)), spec((h, p)), spec((h, p)),
                spec((p, h)), spec((p, h))]
    out_shape = (jax.ShapeDtypeStruct((g, kl, kl), BF16), jax.ShapeDtypeStruct((g, 2 * p, kl), BF16),
                 jax.ShapeDtypeStruct((g, kl, 2 * p), BF16),
                 jax.ShapeDtypeStruct((g, p, 8), F32), jax.ShapeDtypeStruct((g, p, 8), F32))
    out_specs = (spec((kl, kl)), spec((2 * p, kl)), spec((kl, 2 * p)), spec((p, 8)), spec((p, 8)))
    return pl.pallas_call(
        functools.partial(_prep_kernel, chunk=chunk, width=width, n_pow=n_pow, short=short),
        grid=(g // gb,), in_specs=in_specs, out_specs=out_specs, out_shape=out_shape,
        compiler_params=_cparams(("arbitrary",)), name=f"s5_prep_{chunk}")(*args)


def _phase_copies(x5_hbm, buf, sem, step, slot, *, per_step, to_hbm=False):
    nseq, nq, _, chunk, _ = x5_hbm.shape
    copies = []
    for ph in range(per_step):
        gp = step * per_step + ph
        r, s = gp // chunk, gp % chunk
        for seq in range(nseq):
            hbm = x5_hbm.at[seq, :, r, s, :]
            vmem = buf.at[slot, ph, pl.ds(seq * nq, nq), :]
            copies.append(pltpu.make_async_copy(vmem, hbm, sem.at[slot]) if to_hbm
                          else pltpu.make_async_copy(hbm, vmem, sem.at[slot]))
    return copies


def _grid_step():
    return (pl.program_id(0) * pl.num_programs(1) + pl.program_id(1),
            pl.num_programs(0) * pl.num_programs(1))


def _fetch_phases(x5_hbm, buf, sem, *, per_step):
    step, n_steps = _grid_step()
    slot = step % 2
    copies = functools.partial(_phase_copies, x5_hbm, buf, sem, per_step=per_step)

    @pl.when(step == 0)
    def _():
        for c in copies(0, 0):
            c.start()

    @pl.when(step + 1 < n_steps)
    def _():
        for c in copies(step + 1, 1 - slot):
            c.start()

    for c in copies(step, slot):
        c.wait()
    return slot


def _store_phases(o5_hbm, buf, sem, fill, *, per_step):
    step, n_steps = _grid_step()
    slot = step % 2
    copies = functools.partial(_phase_copies, o5_hbm, buf, sem, per_step=per_step, to_hbm=True)

    @pl.when(step >= 2)
    def _():
        for c in copies(step - 2, slot):
            c.wait()

    fill(slot)
    for c in copies(step, slot):
        c.start()

    @pl.when(step == n_steps - 1)
    def _():
        @pl.when(step >= 1)
        def _():
            for c in copies(step - 1, 1 - slot):
                c.wait()
        for c in copies(step, slot):
            c.wait()


INPROJ_PHASES = 2


def _inproj0_kernel(x5_hbm, g_ref, w_ref, u_ref, gate_ref, xbuf, sem):
    slot = _fetch_phases(x5_hbm, xbuf, sem, per_step=INPROJ_PHASES)
    g = g_ref[...]
    xns = [_rms_norm(xbuf[slot, ph], g).astype(BF16) for ph in range(INPROJ_PHASES)]
    ress = [lax.dot_general(w_ref[...], xn, (((1,), (1,)), ((), ())), preferred_element_type=F32)
            for xn in xns]
    for ph, res in enumerate(ress):
        u_ref[:, SSM_GROUP * ph:SSM_GROUP * (ph + 1), :] = (
            res[:D_INNER].reshape(SSM_GROUPS, SSM_GROUP, -1).astype(BF16))
        gate_ref[ph] = res[D_INNER:].astype(BF16)


def _inproj0(x5, g, w_t):
    nseq, nq, n_tiles, chunk, _ = x5.shape
    tc = nseq * nq
    nc = tc * n_tiles
    ph = INPROJ_PHASES
    return pl.pallas_call(
        _inproj0_kernel,
        grid=(n_tiles, chunk // ph),
        in_specs=[pl.BlockSpec(memory_space=pl.ANY),
                  pl.BlockSpec((1, D_MODEL), lambda i, s: (0, 0)),
                  pl.BlockSpec((2 * D_INNER, D_MODEL), lambda i, s: (0, 0))],
        out_specs=(pl.BlockSpec((SSM_GROUPS, SSM_GROUP * ph, tc), lambda i, s: (0, s, i)),
                   pl.BlockSpec((ph, D_INNER, tc), lambda i, s: (s, 0, i))),
        out_shape=(jax.ShapeDtypeStruct((SSM_GROUPS, SSM_GROUP * chunk, nc), BF16),
                   jax.ShapeDtypeStruct((chunk, D_INNER, nc), BF16)),
        scratch_shapes=[pltpu.VMEM((2, ph, tc, D_MODEL), F32), pltpu.SemaphoreType.DMA((2,))],
        compiler_params=_cparams(("arbitrary", "arbitrary")), name=f"l0_inproj_{chunk}")(x5, g, w_t)


def _ssm_kernel(*refs, chunk, n_tiles, segs, carried):
    if carried:
        (u_ref, m_ref, w_ref, v_ref, dre_ref, dim_ref, d_ref, h0re_ref, h0im_ref,
         y_ref, hre_ref, him_ref, sre_ref, sim_ref) = refs
    else:
        (u_ref, m_ref, w_ref, v_ref, dre_ref, dim_ref, d_ref, y_ref, hfin_ref,
         sre_ref, sim_ref, ere_ref, eim_ref, tt_ref) = refs
    nc = u_ref.shape[-1]
    tl = nc // n_tiles
    p = SSM_STATE
    gb = u_ref.shape[0]
    groups = range(gb)

    def tile(r):
        return slice(r * tl, (r + 1) * tl)

    def cmul(ar, ai, br, bi):
        return ar * br - ai * bi, ar * bi + ai * br

    for j in groups:
        z = jnp.dot(w_ref[j], u_ref[j], preferred_element_type=F32)
        sre_ref[j] = z[:p]
        sim_ref[j] = z[p:]
    if carried:
        h0re = h0re_ref[...].T
        h0im = h0im_ref[...].T
        for j in groups:
            dr, di = cmul(dre_ref[j][:, 0:1], dim_ref[j][:, 0:1],
                          h0re[p * j:p * (j + 1)], h0im[p * j:p * (j + 1)])
            sre_ref[j] = sre_ref[j] + dr
            sim_ref[j] = sim_ref[j] + di
        hre_ref[...] = sre_ref[...].reshape(gb * p, nc).T
        him_ref[...] = sim_ref[...].reshape(gb * p, nc).T
    else:
        lane = lax.broadcasted_iota(jnp.int32, (1, tl), 1)
        posq = lane & (segs - 1)
        for r in range(1, n_tiles):
            for j in groups:
                dr, di = cmul(dre_ref[j][:, 0:1], dim_ref[j][:, 0:1],
                              sre_ref[j, :, tile(r - 1)], sim_ref[j, :, tile(r - 1)])
                sre_ref[j, :, tile(r)] = sre_ref[j, :, tile(r)] + dr
                sim_ref[j, :, tile(r)] = sim_ref[j, :, tile(r)] + di
        for j in groups:
            ere_ref[j] = sre_ref[j, :, tile(n_tiles - 1)]
            eim_ref[j] = sim_ref[j, :, tile(n_tiles - 1)]
        col0 = n_tiles.bit_length() - 1
        for k in range(segs.bit_length() - 1):
            keep = posq >= (1 << k)
            for j in groups:
                ar = jnp.where(keep, dre_ref[j][:, col0 + k:col0 + k + 1], 0.0)
                ai = jnp.where(keep, dim_ref[j][:, col0 + k:col0 + k + 1], 0.0)
                t_re, t_im = ere_ref[j], eim_ref[j]
                dr, di = cmul(ar, ai, pltpu.roll(t_re, 1 << k, 1), pltpu.roll(t_im, 1 << k, 1))
                ere_ref[j] = t_re + dr
                eim_ref[j] = t_im + di
        first = posq == 0
        nseq = tl // segs
        for j in groups:
            tt_ref[j] = jnp.concatenate([ere_ref[j], eim_ref[j]], axis=0).T
            hfin_ref[j] = tt_ref[j, pl.ds(segs - 1, nseq, stride=segs), :]
            ere_ref[j] = jnp.where(first, 0.0, pltpu.roll(ere_ref[j], 1, 1))
            eim_ref[j] = jnp.where(first, 0.0, pltpu.roll(eim_ref[j], 1, 1))
        for j in groups:
            a_re, a_im = dre_ref[j][:, 0:1], dim_ref[j][:, 0:1]
            pw_re, pw_im = a_re, a_im
            e_re, e_im = ere_ref[j], eim_ref[j]
            for r in range(n_tiles):
                dr, di = cmul(pw_re, pw_im, e_re, e_im)
                sre_ref[j, :, tile(r)] = sre_ref[j, :, tile(r)] + dr
                sim_ref[j, :, tile(r)] = sim_ref[j, :, tile(r)] + di
                pw_re, pw_im = cmul(pw_re, pw_im, a_re, a_im)
    for j in groups:
        u = u_ref[j]
        if carried:
            pre_re, pre_im = h0re[p * j:p * (j + 1)], h0im[p * j:p * (j + 1)]
        elif n_tiles == 1:
            pre_re, pre_im = ere_ref[j], eim_ref[j]
        else:
            pre_re = jnp.concatenate([ere_ref[j], sre_ref[j, :, :nc - tl]], axis=1)
            pre_im = jnp.concatenate([eim_ref[j], sim_ref[j, :, :nc - tl]], axis=1)
        prev = jnp.concatenate([pre_re, pre_im], axis=0).astype(BF16)
        y = (jnp.dot(m_ref[j], u, preferred_element_type=F32)
             + jnp.dot(v_ref[j], prev, preferred_element_type=F32)
             + d_ref[j] * u.astype(F32))
        y_ref[:, SSM_GROUP * j:SSM_GROUP * (j + 1), :] = (
            y.reshape(chunk, SSM_GROUP, nc).astype(BF16))


def _ssm(u2, m, w, v, dre, dim, d_col, h0, chunk, n_tiles, segs):
    g, kl, nc = u2.shape
    carried = h0 is not None
    gb, p = (CARRIED_GROUP_BLOCK if carried else GROUP_BLOCK), SSM_STATE
    tl = nc // n_tiles
    wspec = lambda shape: pl.BlockSpec((gb,) + shape, lambda a: (a, 0, 0))
    in_specs = [wspec((kl, nc)), wspec((kl, kl)), wspec((2 * p, kl)), wspec((kl, 2 * p)),
                wspec((p, 8)), wspec((p, 8)), wspec((kl, 1))]
    args = [u2, m, w, v, dre, dim, d_col]
    scratch = [pltpu.VMEM((gb, p, nc), F32), pltpu.VMEM((gb, p, nc), F32)]
    y_spec = pl.BlockSpec((chunk, SSM_GROUP * gb, nc), lambda a: (0, a, 0))
    y_shape = jax.ShapeDtypeStruct((chunk, D_INNER, nc), BF16)
    if carried:
        assert n_tiles == 1
        state_spec = pl.BlockSpec((nc, gb * p), lambda a: (0, a))
        in_specs += [state_spec] * 2
        args += list(h0)
        out_specs = (y_spec, state_spec, state_spec)
        out_shape = (y_shape,) + (jax.ShapeDtypeStruct((nc, g * p), F32),) * 2
    else:
        nseq = tl // segs
        scratch += [pltpu.VMEM((gb, p, tl), F32), pltpu.VMEM((gb, p, tl), F32),
                    pltpu.VMEM((gb, tl, 2 * p), F32)]
        out_specs = (y_spec, wspec((nseq, 2 * p)))
        out_shape = (y_shape, jax.ShapeDtypeStruct((g, nseq, 2 * p), F32))
    return pl.pallas_call(
        functools.partial(_ssm_kernel, chunk=chunk, n_tiles=n_tiles, segs=segs, carried=carried),
        grid=(g // gb,), in_specs=in_specs, out_specs=out_specs, out_shape=out_shape,
        scratch_shapes=scratch,
        compiler_params=_cparams(("arbitrary",)), name=f"s5_scan_{chunk}")(*args)


POST_PHASES = 2


def _post0_kernel(y_ref, gate_ref, x5_hbm, wglu_ref, bglu_ref, wout_ref, o_ref, xbuf, sem, *scratch,
                  phase_major_out):
    slot = _fetch_phases(x5_hbm, xbuf, sem, per_step=POST_PHASES)
    phases = range(POST_PHASES)
    ys = [y_ref[ph].astype(F32) for ph in phases]
    ys = [0.5 * y * (1.0 + lax.erf(y * math.sqrt(0.5))) for y in ys]
    zs = [jnp.dot(wglu_ref[...], y.astype(BF16), preferred_element_type=F32) + bglu_ref[...] for y in ys]
    ys = [y * _sigmoid(z) * _silu(gate_ref[ph].astype(F32)) for ph, y, z in zip(phases, ys, zs)]
    outs = [jnp.dot(wout_ref[...], y.astype(BF16), preferred_element_type=F32) for y in ys]
    if phase_major_out:
        for ph, o in zip(phases, outs):
            o_ref[ph] = xbuf[slot, ph] + o.T
    else:
        obuf, osem = scratch

        def fill(oslot):
            for ph, o in zip(phases, outs):
                obuf[oslot, ph] = xbuf[slot, ph] + o.T

        _store_phases(o_ref, obuf, osem, fill, per_step=POST_PHASES)


def _post0(y2, gate2, x5, wglu_t, bglu_col, wout_t, phase_major_out):
    nseq, nq, n_tiles, chunk, _ = x5.shape
    tc = nseq * nq
    nc = tc * n_tiles
    ph = POST_PHASES
    scratch = [pltpu.VMEM((2, ph, tc, D_MODEL), F32), pltpu.SemaphoreType.DMA((2,))]
    if phase_major_out:
        out_shape = jax.ShapeDtypeStruct((chunk, nc, D_MODEL), F32)
        out_spec = pl.BlockSpec((ph, tc, D_MODEL), lambda r, s: (s, r, 0))
    else:
        out_shape = jax.ShapeDtypeStruct(x5.shape, F32)
        out_spec = pl.BlockSpec(memory_space=pl.ANY)
        scratch = scratch * 2
    return pl.pallas_call(
        functools.partial(_post0_kernel, phase_major_out=phase_major_out),
        grid=(n_tiles, chunk // ph),
        in_specs=[pl.BlockSpec((ph, D_INNER, tc), lambda r, s: (s, 0, r)),
                  pl.BlockSpec((ph, D_INNER, tc), lambda r, s: (s, 0, r)),
                  pl.BlockSpec(memory_space=pl.ANY),
                  pl.BlockSpec((D_INNER, D_INNER), lambda r, s: (0, 0)),
                  pl.BlockSpec((D_INNER, 1), lambda r, s: (0, 0)),
                  pl.BlockSpec((D_MODEL, D_INNER), lambda r, s: (0, 0))],
        out_specs=out_spec, out_shape=out_shape,
        scratch_shapes=scratch,
        compiler_params=_cparams(("arbitrary", "arbitrary")), name=f"l0_post_{chunk}")(
            y2, gate2, x5, wglu_t, bglu_col, wout_t)


def _layer_norm_act(z, gate, lng, lnb):
    mu = jnp.mean(z, axis=-1, keepdims=True)
    zc = z - mu
    var = jnp.mean(zc * zc, axis=-1, keepdims=True)
    zn = zc * lax.rsqrt(var + EPS) * lng + lnb
    return _silu(zn) * _silu(gate)


CONV_COLS = 256
HIST_CHUNKS = 8


def _conv_taps(lp):
    taps = {}
    for t in range(lp):
        for k in range(CONV_WIDTH):
            o = t + k - CONV_HIST
            delta = (-o + lp - 1) // lp if o < 0 else 0
            taps[t, k] = (o + lp * delta, delta)
    return taps


def _conv_prompt_kernel(x4_hbm, g_ref, w_ref, dww_ref, dwb_ref, lng_ref, lnb_ref, wout_ref, gf_ref,
                        y4_hbm, cst_ref, xn_ref, vbuf_ref, vsh_ref, gate_ref, zc_ref,
                        xbuf, xsem, ybuf, ysem, *, lp, ct, tiles, n_seq):
    j = pl.program_id(1)
    hist = HIST_CHUNKS
    taps = _conv_taps(lp)
    shifted = sorted({sd for sd in taps.values() if sd[1] > 0})
    shift_slot = {sd: i for i, sd in enumerate(shifted)}
    n_blocks = D_INNER // CONV_COLS
    step = pl.program_id(0) * tiles + j
    n_steps = n_seq * tiles
    buf = step % 2

    def tile_copies(hbm, vmem, sem, stp, slt, to_hbm):
        seq, tile = stp // tiles, stp % tiles
        pairs = [(hbm.at[seq, pl.ds(tile * ct, ct), t, :], vmem.at[slt, t]) for t in range(lp)]
        return [pltpu.make_async_copy(v, h, sem.at[slt]) if to_hbm else
                pltpu.make_async_copy(h, v, sem.at[slt]) for h, v in pairs]

    x_copies = functools.partial(tile_copies, x4_hbm, xbuf, xsem, to_hbm=False)
    y_copies = functools.partial(tile_copies, y4_hbm, ybuf, ysem, to_hbm=True)

    @pl.when(step == 0)
    def _():
        for c in x_copies(0, 0):
            c.start()

    @pl.when(step + 1 < n_steps)
    def _():
        for c in x_copies(step + 1, 1 - buf):
            c.start()

    for c in x_copies(step, buf):
        c.wait()

    @pl.when(j == 0)
    def _():
        vbuf_ref[:, 0:hist, :] = jnp.zeros((lp, hist, D_INNER), F32)

    g = g_ref[...]
    for s in range(lp):
        xn_ref[ct * s:ct * (s + 1), :] = _rms_norm(xbuf[buf, s], g).astype(BF16)

    def lanes(start):
        return pl.ds(pl.multiple_of(start, CONV_COLS), CONV_COLS)

    def project(c):
        xn = xn_ref[...]
        a = jnp.dot(xn, w_ref[:, lanes(c * CONV_COLS)], preferred_element_type=F32)
        b = jnp.dot(xn, w_ref[:, lanes(D_INNER + c * CONV_COLS)], preferred_element_type=F32)
        cols = lanes(c * CONV_COLS)
        gate_ref[:, cols] = jnp.dot(xn, w_ref[:, lanes(2 * D_INNER + c * CONV_COLS)],
                                    preferred_element_type=F32)
        v = a * _sigmoid(b)
        for s in range(lp):
            vbuf_ref[s, hist:hist + ct, cols] = v[ct * s:ct * (s + 1)]
        for (s, delta), i in shift_slot.items():
            vsh_ref[i, :, cols] = vbuf_ref[s, hist - delta:hist - delta + ct, cols]

    def conv(c):
        cols = lanes(c * CONV_COLS)
        for t in range(lp):
            acc = jnp.broadcast_to(dwb_ref[:, cols], (ct, CONV_COLS))
            for k in range(CONV_WIDTH):
                s, delta = taps[t, k]
                slab = vsh_ref[shift_slot[s, delta], :, cols] if delta else vbuf_ref[s, hist:hist + ct, cols]
                acc = acc + dww_ref[k:k + 1, cols] * slab
            zc_ref[ct * t:ct * (t + 1), cols] = acc

    project(0)

    def block(c, carry):
        conv(c)
        project(c + 1)
        return carry

    lax.fori_loop(0, n_blocks - 1, block, 0)
    conv(n_blocks - 1)
    zact = _layer_norm_act(zc_ref[...], gate_ref[...], lng_ref[...], lnb_ref[...]).astype(BF16)
    o = jnp.dot(zact, wout_ref[...], preferred_element_type=F32)
    gf = gf_ref[...]

    @pl.when(step >= 2)
    def _():
        for c in y_copies(step - 2, buf):
            c.wait()

    for t in range(lp):
        ybuf[buf, t] = _rms_norm(xbuf[buf, t] + o[ct * t:ct * (t + 1)], gf)
    for c in y_copies(step, buf):
        c.start()

    @pl.when(step == n_steps - 1)
    def _():
        if n_steps >= 2:
            for c in y_copies(step - 1, 1 - buf):
                c.wait()
        for c in y_copies(step, buf):
            c.wait()

    @pl.when(j == tiles - 1)
    def _():
        for i in range(CONV_HIST):
            tok = lp * ct - CONV_HIST + i
            cst_ref[i:i + 1, :] = vbuf_ref[tok % lp, hist + tok // lp:hist + tok // lp + 1, :]

    vbuf_ref[:, 0:hist, :] = vbuf_ref[:, ct:ct + hist, :]


def _layer1_prompt(x4, g, w_in, dw_w, dw_b, ln_g, ln_b, w_out, gf, ct):
    n, cps, lp, _ = x4.shape
    tiles = cps // ct
    n_shift = len({sd for sd in _conv_taps(lp).values() if sd[1] > 0})
    const = lambda shape: pl.BlockSpec(shape, lambda a, j: (0,) * len(shape), pipeline_mode=pl.Buffered(1))
    return pl.pallas_call(
        functools.partial(_conv_prompt_kernel, lp=lp, ct=ct, tiles=tiles, n_seq=n),
        grid=(n, tiles),
        in_specs=[pl.BlockSpec(memory_space=pl.ANY),
                  const((1, D_MODEL)), const((D_MODEL, 3 * D_INNER)),
                  const((CONV_WIDTH, D_INNER)), const((1, D_INNER)),
                  const((1, D_INNER)), const((1, D_INNER)),
                  const((D_INNER, D_MODEL)), const((1, D_MODEL))],
        out_specs=(pl.BlockSpec(memory_space=pl.ANY),
                   pl.BlockSpec((None, CONV_HIST, D_INNER), lambda a, j: (a, 0, 0))),
        out_shape=(jax.ShapeDtypeStruct(x4.shape, F32),
                   jax.ShapeDtypeStruct((n, CONV_HIST, D_INNER), F32)),
        scratch_shapes=[pltpu.VMEM((lp * ct, D_MODEL), BF16),
                        pltpu.VMEM((lp, HIST_CHUNKS + ct, D_INNER), F32),
                        pltpu.VMEM((n_shift, ct, D_INNER), F32),
                        pltpu.VMEM((lp * ct, D_INNER), F32),
                        pltpu.VMEM((lp * ct, D_INNER), F32),
                        pltpu.VMEM((2, lp, ct, D_MODEL), F32), pltpu.SemaphoreType.DMA((2,)),
                        pltpu.VMEM((2, lp, ct, D_MODEL), F32), pltpu.SemaphoreType.DMA((2,))],
        compiler_params=_cparams(("arbitrary", "arbitrary")), name="l1_prompt")(
            x4, g, w_in, dw_w, dw_b, ln_g, ln_b, w_out, gf)


def _inproj1_kernel(x_ref, g_ref, w_ref, v_ref, gate_ref):
    xn = _rms_norm(x_ref[...], g_ref[...]).astype(BF16)
    abg = jnp.dot(xn, w_ref[...], preferred_element_type=F32)
    v_ref[...] = abg[:, :D_INNER] * _sigmoid(abg[:, D_INNER:2 * D_INNER])
    gate_ref[...] = abg[:, 2 * D_INNER:]


def _inproj1(x1, g, w_in, tm):
    r = x1.shape[0]
    const = lambda shape: pl.BlockSpec(shape, lambda i: (0,) * len(shape))
    return pl.pallas_call(
        _inproj1_kernel, grid=(r // tm,),
        in_specs=[pl.BlockSpec((tm, D_MODEL), lambda i: (i, 0)), const((1, D_MODEL)),
                  const((D_MODEL, 3 * D_INNER))],
        out_specs=(pl.BlockSpec((tm, D_INNER), lambda i: (i, 0)),) * 2,
        out_shape=(jax.ShapeDtypeStruct((r, D_INNER), F32),) * 2,
        compiler_params=_cparams(("arbitrary",)), name="l1_inproj_sample")(x1, g, w_in)


SAMPLE_SEQ_TILE = 16
SAMPLE_COLS = 1024


def _conv_sample_kernel(cache_ref, v_ref, gate_ref, dww_ref, dwb_ref, lng_ref, lnb_ref,
                        z_ref, cst_ref, acc_ref, *, steps):
    ns = cache_ref.shape[0]
    for t in range(steps):
        for c0 in range(0, D_INNER, SAMPLE_COLS):
            acc = jnp.broadcast_to(dwb_ref[:, c0:c0 + SAMPLE_COLS], (ns, SAMPLE_COLS))
            for k in range(CONV_WIDTH):
                jrow = t + k
                if jrow < CONV_HIST:
                    src = cache_ref[:, jrow * D_INNER + c0:jrow * D_INNER + c0 + SAMPLE_COLS]
                else:
                    src = v_ref[jrow - CONV_HIST, :, c0:c0 + SAMPLE_COLS]
                acc = acc + dww_ref[k:k + 1, c0:c0 + SAMPLE_COLS] * src
            acc_ref[t, :, c0:c0 + SAMPLE_COLS] = acc
        z_ref[t] = _layer_norm_act(acc_ref[t], gate_ref[t], lng_ref[...], lnb_ref[...]).astype(BF16)
    keep = CONV_HIST - steps
    cst_ref[:, 0:keep * D_INNER] = cache_ref[:, steps * D_INNER:CONV_HIST * D_INNER]
    for t in range(steps):
        cst_ref[:, (keep + t) * D_INNER:(keep + t + 1) * D_INNER] = v_ref[t]


def _conv_sample(cache2d, v3, gate3, dw_w, dw_b, ln_g, ln_b):
    steps, n, _ = v3.shape
    ns = SAMPLE_SEQ_TILE
    const = lambda shape: pl.BlockSpec(shape, lambda i: (0,) * len(shape))
    return pl.pallas_call(
        functools.partial(_conv_sample_kernel, steps=steps), grid=(n // ns,),
        in_specs=[pl.BlockSpec((ns, CONV_HIST * D_INNER), lambda i: (i, 0)),
                  pl.BlockSpec((steps, ns, D_INNER), lambda i: (0, i, 0)),
                  pl.BlockSpec((steps, ns, D_INNER), lambda i: (0, i, 0)),
                  const((CONV_WIDTH, D_INNER)), const((1, D_INNER)), const((1, D_INNER)),
                  const((1, D_INNER))],
        out_specs=(pl.BlockSpec((steps, ns, D_INNER), lambda i: (0, i, 0)),
                   pl.BlockSpec((ns, CONV_HIST * D_INNER), lambda i: (i, 0))),
        out_shape=(jax.ShapeDtypeStruct((steps, n, D_INNER), BF16),
                   jax.ShapeDtypeStruct((n, CONV_HIST * D_INNER), F32)),
        scratch_shapes=[pltpu.VMEM((steps, ns, D_INNER), F32)],
        compiler_params=_cparams(("arbitrary",)), name="l1_conv_sample")(
            cache2d, v3, gate3, dw_w, dw_b, ln_g, ln_b)


def _out1_kernel(z_ref, x_ref, w_ref, g_ref, y_ref):
    x2 = x_ref[...] + jnp.dot(z_ref[...], w_ref[...], preferred_element_type=F32)
    y_ref[...] = _rms_norm(x2, g_ref[...])


def _out1(z, x1, w_out, g, tm):
    r = x1.shape[0]
    phases = x1.shape[1] // D_MODEL
    const = lambda shape: pl.BlockSpec(shape, lambda i, s: (0,) * len(shape))
    return pl.pallas_call(
        _out1_kernel, grid=(r // tm, phases),
        in_specs=[pl.BlockSpec((tm, D_INNER), lambda i, s: (i, s)),
                  pl.BlockSpec((tm, D_MODEL), lambda i, s: (i, s)),
                  const((D_INNER, D_MODEL)), const((1, D_MODEL))],
        out_specs=pl.BlockSpec((tm, D_MODEL), lambda i, s: (i, s)),
        out_shape=jax.ShapeDtypeStruct((r, phases * D_MODEL), F32),
        compiler_params=_cparams(("arbitrary", "arbitrary")), name="l1_out")(z, x1, w_out, g)


WEIGHT_BLOCK = 1024


def _to_bf16_kernel(w_ref, o_ref, *, transpose):
    w = w_ref[...]
    o_ref[...] = (w.T if transpose else w).astype(BF16)


def _to_bf16(w, transpose):
    r, c = w.shape
    b = WEIGHT_BLOCK
    out_spec = pl.BlockSpec((b, b), (lambda i, j: (j, i)) if transpose else (lambda i, j: (i, j)))
    return pl.pallas_call(
        functools.partial(_to_bf16_kernel, transpose=transpose), grid=(r // b, c // b),
        in_specs=[pl.BlockSpec((b, b), lambda i, j: (i, j))], out_specs=out_spec,
        out_shape=jax.ShapeDtypeStruct((c, r) if transpose else (r, c), BF16),
        compiler_params=_cparams(("arbitrary", "arbitrary")), name="weight_bf16")(w)


def kernel(x_prompt, x_sample, state_ssm_re, state_ssm_im, cache_conv, norm_g, final_norm_g, ssm_w_in, ssm_a_re, ssm_a_im, ssm_log_dt, ssm_b_re, ssm_b_im, ssm_c_re, ssm_c_im, ssm_d, ssm_w_glu, ssm_b_glu, ssm_w_out, conv_w_in, conv_dw_w, conv_dw_b, conv_ln_g, conv_ln_b, conv_w_out):
    n_p, t_p, _ = x_prompt.shape
    n_s, t_s, _ = x_sample.shape
    g, p = SSM_GROUPS, SSM_STATE

    w_in0_t = _to_bf16(ssm_w_in[0], transpose=True)
    w_glu_t = _to_bf16(ssm_w_glu[0], transpose=True)
    b_glu_col = ssm_b_glu[0].reshape(D_INNER, 1)
    w_out0_t = _to_bf16(ssm_w_out[0], transpose=True)
    w_in1 = _to_bf16(conv_w_in[0], transpose=False)
    w_out1 = _to_bf16(conv_w_out[0], transpose=False)
    g0 = norm_g[0].reshape(1, D_MODEL)
    g1 = norm_g[1].reshape(1, D_MODEL)
    gf = final_norm_g.reshape(1, D_MODEL)
    dw_w, dw_b = conv_dw_w[0], conv_dw_b[0].reshape(1, D_INNER)
    ln_g, ln_b = conv_ln_g[0].reshape(1, D_INNER), conv_ln_b[0].reshape(1, D_INNER)

    def layer0(x5, ops, h0, phase_major_out):
        _, segs, n_tiles, chunk, _ = x5.shape
        m, w, v, dre, dim = ops
        d_col = jnp.tile(ssm_d[0].reshape(g, 1, SSM_GROUP), (1, chunk, 1)).reshape(g, chunk * SSM_GROUP, 1)
        u2, gate2 = _inproj0(x5, g0, w_in0_t)
        y2, *states = _ssm(u2, m, w, v, dre, dim, d_col, h0, chunk, n_tiles, segs)
        x1 = _post0(y2, gate2, x5, w_glu_t, b_glu_col, w_out0_t, phase_major_out)
        return x1, states

    lp = PROMPT_CHUNK
    cps = t_p // lp
    segs = cps // SCAN_TILES
    assert t_s <= lp and lp % t_s == 0
    ops_p = _ssm_prep(ssm_a_re[0], ssm_a_im[0], ssm_log_dt[0], ssm_b_re[0], ssm_b_im[0],
                      ssm_c_re[0], ssm_c_im[0], lp, cps.bit_length() - 1, t_s)
    x1p, (hfin,) = layer0(x_prompt.reshape(n_p, segs, SCAN_TILES, lp, D_MODEL), ops_p, None, False)
    hfin = jnp.transpose(hfin, (1, 0, 2))
    ssm_re_p, ssm_im_p = hfin[None, :, :, :p], hfin[None, :, :, p:]
    y4, conv_p = _layer1_prompt(x1p.reshape(n_p, cps, lp, D_MODEL), g1, w_in1, dw_w, dw_b, ln_g, ln_b,
                                w_out1, gf, L1_CHUNK_ROWS)
    y_prompt = y4.reshape(n_p, t_p, D_MODEL)

    h0 = (state_ssm_re[0].reshape(n_s, g * p), state_ssm_im[0].reshape(n_s, g * p))
    m_p, w_p, v_p, dre_p, dim_p = ops_p
    kl_s = SSM_GROUP * t_s
    short_decay = lambda d: jnp.broadcast_to(d[:, :, 7:8], d.shape)
    ops_s = (m_p[:, :kl_s, :kl_s], w_p[:, :, SSM_GROUP * lp - kl_s:], v_p[:, :kl_s, :],
             short_decay(dre_p), short_decay(dim_p))
    x1s, (hre_s, him_s) = layer0(x_sample.reshape(1, n_s, 1, t_s, D_MODEL), ops_s, h0, True)
    ssm_re_s = hre_s.reshape(1, n_s, g, p)
    ssm_im_s = him_s.reshape(1, n_s, g, p)
    x1s = x1s.reshape(t_s * n_s, D_MODEL)
    v_s, gate_s = _inproj1(x1s, g1, w_in1, tm=256)
    zs, conv_s = _conv_sample(cache_conv[0].reshape(n_s, CONV_HIST * D_INNER),
                              v_s.reshape(t_s, n_s, D_INNER), gate_s.reshape(t_s, n_s, D_INNER),
                              dw_w, dw_b, ln_g, ln_b)
    y_s = _out1(zs.reshape(t_s * n_s, D_INNER), x1s, w_out1, gf, tm=t_s * n_s)
    y_sample = jnp.transpose(y_s.reshape(t_s, n_s, D_MODEL), (1, 0, 2))

    return (y_prompt, y_sample, ssm_re_p, ssm_im_p, conv_p[None],
            ssm_re_s, ssm_im_s, conv_s.reshape(1, n_s, CONV_HIST, D_INNER))
```
